```python
import math
import jax, jax.numpy as jnp
from jax import lax
import numpy as np

D_MODEL = 1024
BATCH = 8
SEQ = 2048
DEPTH = 1

CHUNK = 64
Q_BLOCK = 128
EPS = 1e-6
N_HEADS = 16
N_KV_HEADS = 4
HEAD_DIM = 64
ROPE_DIM = HEAD_DIM // 4
ROPE_THETA = 500000.0
IDX_HEADS = 8
IDX_DIM = 64
TOPK_MAX = 256
SSD_D_INNER = 2 * D_MODEL
SSD_HEAD_DIM = 64
SSD_HEADS = SSD_D_INNER // SSD_HEAD_DIM
SSD_GROUPS = 4
SSD_STATE = 128
SSD_CONV = 4
SSD_XBC = SSD_D_INNER + 2 * SSD_GROUPS * SSD_STATE
N_EXPERT_GROUPS = 4
EXPERTS_PER_GROUP = 8
EXPERT_TOPK = 2
EXPERT_HIDDEN = 256
Q_W = N_HEADS * HEAD_DIM
KV_W = N_KV_HEADS * HEAD_DIM
IQ_W = IDX_HEADS * IDX_DIM
SPLITS = (Q_W, KV_W, KV_W, IQ_W, IDX_DIM, IDX_HEADS, SSD_D_INNER, SSD_XBC, SSD_HEADS, D_MODEL, D_MODEL)
IN_PROJ_W = sum(SPLITS)

kernel_name = "hybrid_dsa_ssd_hiermoe_block"


def rmsnorm(x, g):
    xf = x.astype(jnp.float32)
    y = xf * lax.rsqrt(jnp.mean(xf * xf, axis=-1, keepdims=True) + EPS)
    return (y * g.astype(jnp.float32)).astype(x.dtype)


def rope_partial(x, pos):
    half = ROPE_DIM // 2
    inv = ROPE_THETA ** (-jnp.arange(half, dtype=jnp.float32) * 2.0 / ROPE_DIM)
    ang = pos.astype(jnp.float32)[:, None] * inv[None, :]
    cos = jnp.cos(ang)[:, None, :]
    sin = jnp.sin(ang)[:, None, :]
    xr = x[..., :ROPE_DIM].astype(jnp.float32)
    x1, x2 = xr[..., :half], xr[..., half:]
    rot = jnp.concatenate([x1 * cos - x2 * sin, x2 * cos + x1 * sin], axis=-1)
    return jnp.concatenate([rot.astype(x.dtype), x[..., ROPE_DIM:]], axis=-1)


def dsa_attention(q, k, v, iq, ik, iw):
    Bsz, T = q.shape[0], q.shape[1]
    topk = min(TOPK_MAX, T // 4)
    nblk = T // Q_BLOCK
    grp = N_HEADS // N_KV_HEADS
    key_chunk = jnp.arange(T) // CHUNK
    ikf = ik.astype(jnp.float32)

    def to_blocks(a):
        return a.reshape(Bsz, nblk, Q_BLOCK, *a.shape[2:]).swapaxes(0, 1)

    def one_block(args):
        qb, iqb, iwb, start = args
        q_chunk = (start + jnp.arange(Q_BLOCK)) // CHUNK
        admissible = key_chunk[None, :] <= q_chunk[:, None]
        rel = jax.nn.relu(jnp.einsum('bqhd,bsd->bqhs', iqb.astype(jnp.float32), ikf) * IDX_DIM ** -0.5)
        score = jnp.einsum('bqhs,bqh->bqs', rel, iwb.astype(jnp.float32) * IDX_HEADS ** -0.5)
        score = jnp.where(admissible[None], score, -jnp.inf)
        top_s, sel = lax.top_k(score, topk)
        valid = jnp.isfinite(top_s)
        k_sel = jax.vmap(lambda kk, ii: kk[ii])(k, sel)
        v_sel = jax.vmap(lambda vv, ii: vv[ii])(v, sel)
        qg = qb.reshape(Bsz, Q_BLOCK, N_KV_HEADS, grp, HEAD_DIM).astype(jnp.float32)
        logits = jnp.einsum('bqhgd,bqkhd->bqhgk', qg, k_sel.astype(jnp.float32)) * HEAD_DIM ** -0.5
        logits = jnp.where(valid[:, :, None, None, :], logits, -jnp.inf)
        p = jax.nn.softmax(logits, axis=-1)
        o = jnp.einsum('bqhgk,bqkhd->bqhgd', p, v_sel.astype(jnp.float32))
        return o.reshape(Bsz, Q_BLOCK, N_HEADS * HEAD_DIM).astype(q.dtype)

    starts = jnp.arange(nblk, dtype=jnp.int32) * Q_BLOCK
    out = lax.map(one_block, (to_blocks(q), to_blocks(iq), to_blocks(iw), starts))
    return out.swapaxes(0, 1).reshape(Bsz, T, N_HEADS * HEAD_DIM)


def causal_depthwise_conv(x, w, b):
    y = lax.conv_general_dilated(x, w[:, None, :], window_strides=(1,),
                                 padding=[(w.shape[0] - 1, 0)],
                                 dimension_numbers=('NWC', 'WIO', 'NWC'),
                                 feature_group_count=x.shape[-1])
    return y + b


def ssd_chunked(xh, dt, A, Bm, Cm):
    Bsz, T, H, P = xh.shape
    nc = T // CHUNK
    R = H // SSD_GROUPS
    x = (xh * dt[..., None]).reshape(Bsz, nc, CHUNK, SSD_GROUPS, R, P)
    a_cs = jnp.cumsum((dt * A).reshape(Bsz, nc, CHUNK, SSD_GROUPS, R), axis=2)
    Bc = Bm.reshape(Bsz, nc, CHUNK, SSD_GROUPS, SSD_STATE)
    Cc = Cm.reshape(Bsz, nc, CHUNK, SSD_GROUPS, SSD_STATE)
    seg = a_cs[:, :, :, None] - a_cs[:, :, None, :]
    causal = jnp.tril(jnp.ones((CHUNK, CHUNK), dtype=bool))[None, None, :, :, None, None]
    decay = jnp.exp(jnp.where(causal, seg, -jnp.inf))
    cb = jnp.einsum('bclgn,bcsgn->bclsg', Cc, Bc)
    y_diag = jnp.einsum('bclsg,bclsgr,bcsgrp->bclgrp', cb, decay, x)
    decay_to_end = jnp.exp(a_cs[:, :, -1:] - a_cs)
    states = jnp.einsum('bclgn,bclgr,bclgrp->bcgrpn', Bc, decay_to_end, x)
    chunk_decay = jnp.exp(a_cs[:, :, -1])

    def step(h, inp):
        dec, st = inp
        return dec[..., None, None] * h + st, h

    h0 = jnp.zeros((Bsz, SSD_GROUPS, R, P, SSD_STATE), jnp.float32)
    _, prev = lax.scan(step, h0, (chunk_decay.swapaxes(0, 1), states.swapaxes(0, 1)))
    prev = prev.swapaxes(0, 1)
    y_off = jnp.einsum('bclgn,bcgrpn,bclgr->bclgrp', Cc, prev, jnp.exp(a_cs))
    return (y_diag + y_off).reshape(Bsz, T, H, P)


def gated_group_rmsnorm(y, z, g):
    yf = y.astype(jnp.float32) * jax.nn.silu(z.astype(jnp.float32))
    yg = yf.reshape(*yf.shape[:-1], SSD_GROUPS, -1)
    yg = yg * lax.rsqrt(jnp.mean(yg * yg, axis=-1, keepdims=True) + EPS)
    return yg.reshape(yf.shape) * g.astype(jnp.float32)


def hier_moe(h, w_rg, b_rg, w_re, b_re, w_gate, w_up, w_down):
    Bsz, T, D = h.shape
    hf = h.reshape(-1, D)
    g_logits = (hf @ w_rg + b_rg).astype(jnp.float32)
    g_prob = jax.nn.softmax(g_logits, axis=-1)
    g_idx = jnp.argmax(g_logits, axis=-1)
    g_w = jnp.take_along_axis(g_prob, g_idx[:, None], axis=-1)
    e_logits = (hf @ w_re + b_re).astype(jnp.float32).reshape(-1, N_EXPERT_GROUPS, EXPERTS_PER_GROUP)
    e_in = jnp.take_along_axis(e_logits, g_idx[:, None, None], axis=1)[:, 0]
    top_v, top_i = lax.top_k(e_in, EXPERT_TOPK)
    top_p = jax.nn.softmax(top_v, axis=-1)
    within = jnp.sum(jax.nn.one_hot(top_i, EXPERTS_PER_GROUP, dtype=jnp.float32) * top_p[..., None], axis=1)
    combine = jax.nn.one_hot(g_idx, N_EXPERT_GROUPS, dtype=jnp.float32)[:, :, None] * (g_w * within)[:, None, :]
    combine = combine.astype(hf.dtype)
    out = jnp.zeros_like(hf)
    for g in range(N_EXPERT_GROUPS):
        a = jnp.einsum('nd,edf->nef', hf, w_gate[g])
        u = jnp.einsum('nd,edf->nef', hf, w_up[g])
        act = jax.nn.silu(a) * u * combine[:, g, :, None]
        out = out + jnp.einsum('nef,efd->nd', act, w_down[g])
    return out.reshape(Bsz, T, D)


def setup_inputs(seed: int = 0) -> dict:
    key = jax.random.key(seed)
    ks = jax.random.split(key, 24)
    L = DEPTH
    f32 = jnp.float32

    def nrm(k, shape, scale):
        return jax.random.normal(k, shape, f32) * scale

    def gain(k, n):
        return 1.0 + 0.02 * jax.random.normal(k, (L, n), f32)

    dt0 = jnp.exp(jax.random.uniform(ks[8], (L, SSD_HEADS), f32, math.log(1e-3), math.log(1e-1)))
    return {
        "x": nrm(ks[0], (BATCH, SEQ, D_MODEL), 1.0),
        "attn_norm": gain(ks[1], D_MODEL),
        "w_in": nrm(ks[2], (L, D_MODEL, IN_PROJ_W), D_MODEL ** -0.5),
        "q_norm": gain(ks[3], HEAD_DIM),
        "k_norm": gain(ks[4], HEAD_DIM),
        "idx_k_norm": gain(ks[5], IDX_DIM),
        "conv_w": nrm(ks[6], (L, SSD_CONV, SSD_XBC), SSD_CONV ** -0.5),
        "conv_b": nrm(ks[7], (L, SSD_XBC), 0.02),
        "dt_bias": dt0 + jnp.log(-jnp.expm1(-dt0)),
        "a_log": jnp.log(jax.random.uniform(ks[9], (L, SSD_HEADS), f32, 1.0, 16.0)),
        "d_skip": 1.0 + 0.1 * jax.random.normal(ks[10], (L, SSD_HEADS), f32),
        "ssd_norm": gain(ks[11], SSD_D_INNER),
        "w_attn_branch": nrm(ks[12], (L, Q_W, D_MODEL), Q_W ** -0.5),
        "w_ssd_branch": nrm(ks[13], (L, SSD_D_INNER, D_MODEL), SSD_D_INNER ** -0.5),
        "w_out": nrm(ks[14], (L, D_MODEL, D_MODEL), D_MODEL ** -0.5),
        "ffn_norm": gain(ks[15], D_MODEL),
        "w_route_group": nrm(ks[16], (L, D_MODEL, N_EXPERT_GROUPS), D_MODEL ** -0.5),
        "b_route_group": nrm(ks[17], (L, N_EXPERT_GROUPS), 0.01),
        "w_route_expert": nrm(ks[18], (L, D_MODEL, N_EXPERT_GROUPS * EXPERTS_PER_GROUP), D_MODEL ** -0.5),
        "b_route_expert": nrm(ks[19], (L, N_EXPERT_GROUPS * EXPERTS_PER_GROUP), 0.01),
        "w_gate": nrm(ks[20], (L, N_EXPERT_GROUPS, EXPERTS_PER_GROUP, D_MODEL, EXPERT_HIDDEN), D_MODEL ** -0.5),
        "w_up": nrm(ks[21], (L, N_EXPERT_GROUPS, EXPERTS_PER_GROUP, D_MODEL, EXPERT_HIDDEN), D_MODEL ** -0.5),
        "w_down": nrm(ks[22], (L, N_EXPERT_GROUPS, EXPERTS_PER_GROUP, EXPERT_HIDDEN, D_MODEL), EXPERT_HIDDEN ** -0.5),
    }


def reference(x, attn_norm, w_in, q_norm, k_norm, idx_k_norm, conv_w, conv_b, dt_bias, a_log, d_skip,
              ssd_norm, w_attn_branch, w_ssd_branch, w_out, ffn_norm, w_route_group, b_route_group,
              w_route_expert, b_route_expert, w_gate, w_up, w_down):
    Bsz, T, _ = x.shape
    pos = jnp.arange(T)
    offs = np.cumsum(SPLITS)[:-1].tolist()
    for l in range(DEPTH):
        h = rmsnorm(x, attn_norm[l])
        proj = h @ w_in[l]
        (q, k, v, iq, ik, iw, z, xbc, dt_raw, gate_a, gate_b) = jnp.split(proj, offs, axis=-1)
        q = rope_partial(rmsnorm(q.reshape(Bsz, T, N_HEADS, HEAD_DIM), q_norm[l]), pos)
        k = rope_partial(rmsnorm(k.reshape(Bsz, T, N_KV_HEADS, HEAD_DIM), k_norm[l]), pos)
        v = v.reshape(Bsz, T, N_KV_HEADS, HEAD_DIM)
        iq = rope_partial(iq.reshape(Bsz, T, IDX_HEADS, IDX_DIM), pos)
        ik = rope_partial(rmsnorm(ik, idx_k_norm[l])[:, :, None, :], pos)[:, :, 0, :]
        y_attn = dsa_attention(q, k, v, iq, ik, iw)
        xbc = jax.nn.silu(causal_depthwise_conv(xbc, conv_w[l], conv_b[l]))
        xs, bs, cs = jnp.split(xbc, [SSD_D_INNER, SSD_D_INNER + SSD_GROUPS * SSD_STATE], axis=-1)
        xh = xs.reshape(Bsz, T, SSD_HEADS, SSD_HEAD_DIM).astype(jnp.float32)
        dt = jax.nn.softplus(dt_raw.astype(jnp.float32) + dt_bias[l].astype(jnp.float32))
        A = -jnp.exp(a_log[l].astype(jnp.float32))
        y_ssd = ssd_chunked(xh, dt, A,
                            bs.reshape(Bsz, T, SSD_GROUPS, SSD_STATE).astype(jnp.float32),
                            cs.reshape(Bsz, T, SSD_GROUPS, SSD_STATE).astype(jnp.float32))
        y_ssd = y_ssd + d_skip[l].astype(jnp.float32)[:, None] * xh
        y_ssd = gated_group_rmsnorm(y_ssd.reshape(Bsz, T, SSD_D_INNER), z, ssd_norm[l]).astype(x.dtype)
        merged = jax.nn.sigmoid(gate_a) * (y_attn @ w_attn_branch[l]) + jax.nn.sigmoid(gate_b) * (y_ssd @ w_ssd_branch[l])
        x = x + merged @ w_out[l]
        x = x + hier_moe(rmsnorm(x, ffn_norm[l]), w_route_group[l], b_route_group[l], w_route_expert[l],
                         b_route_expert[l], w_gate[l], w_up[l], w_down[l])
    return x
```

```python
import functools
import math

import numpy as np
import jax
import jax.numpy as jnp
from jax import lax
from jax.experimental import pallas as pl
from jax.experimental.pallas import tpu as pltpu

f32 = jnp.float32
bf16 = jnp.bfloat16

D_MODEL = 1024
CHUNK = 64
Q_BLOCK = 128
EPS = 1e-6
N_HEADS = 16
N_KV_HEADS = 4
HEAD_DIM = 64
ROPE_DIM = HEAD_DIM // 4
ROPE_THETA = 500000.0
IDX_HEADS = 8
IDX_DIM = 64
TOPK_MAX = 256
SSD_D_INNER = 2 * D_MODEL
SSD_HEAD_DIM = 64
SSD_HEADS = SSD_D_INNER // SSD_HEAD_DIM
SSD_GROUPS = 4
SSD_STATE = 128
SSD_CONV = 4
SSD_BC = SSD_GROUPS * SSD_STATE
SSD_XBC = SSD_D_INNER + 2 * SSD_BC
N_EXPERT_GROUPS = 4
EXPERTS_PER_GROUP = 8
N_EXPERTS = N_EXPERT_GROUPS * EXPERTS_PER_GROUP
EXPERT_HIDDEN = 256
Q_W = N_HEADS * HEAD_DIM
KV_W = N_KV_HEADS * HEAD_DIM
IQ_W = IDX_HEADS * IDX_DIM
SPLITS = (Q_W, KV_W, KV_W, IQ_W, IDX_DIM, IDX_HEADS, SSD_D_INNER, SSD_XBC, SSD_HEADS, D_MODEL, D_MODEL)

LANES = 128
MAIN_W = Q_W + 2 * KV_W + IQ_W + SSD_D_INNER + SSD_XBC + 2 * D_MODEL
COL_Z = Q_W + 2 * KV_W + IQ_W
COL_XBC = COL_Z + SSD_D_INNER
COL_GA = COL_XBC + SSD_XBC
COL_GB = COL_GA + D_MODEL
SM_IW = IDX_DIM
SM_DT = IDX_DIM + IDX_HEADS
SSD_CHUNK = 256
VMEM_LIMIT = 56 * 1024 * 1024
INT_MIN = -(2 ** 31)


def _cparams(*sem):
    return pltpu.CompilerParams(dimension_semantics=sem, vmem_limit_bytes=VMEM_LIMIT)


def _split_dot(a, b, parts):
    acc = None
    rem = a
    for _ in range(parts):
        piece = rem.astype(bf16)
        d = jnp.dot(piece, b, preferred_element_type=f32)
        acc = d if acc is None else acc + d
        rem = rem - piece.astype(f32)
    return acc


def _inproj_kernel(x_ref, g_ref, w_ref, ws_ref, o_ref, os_ref, h_ref):
    @pl.when(pl.program_id(1) == 0)
    def _():
        x = x_ref[...]
        ms = jnp.mean(x * x, axis=-1, keepdims=True)
        hb = (x * lax.rsqrt(ms + EPS) * g_ref[...]).astype(bf16)
        h_ref[...] = hb
        os_ref[...] = jnp.dot(hb, ws_ref[...], preferred_element_type=f32)

    o_ref[...] = jnp.dot(h_ref[...], w_ref[...], preferred_element_type=f32).astype(bf16)


def _in_proj(x2d, gain, w_main, w_small):
    n = x2d.shape[0]
    tm = min(1024, n)
    tn = 1024
    return pl.pallas_call(
        _inproj_kernel,
        grid=(n // tm, MAIN_W // tn),
        in_specs=[
            pl.BlockSpec((tm, D_MODEL), lambda i, j: (i, 0)),
            pl.BlockSpec((1, D_MODEL), lambda i, j: (0, 0)),
            pl.BlockSpec((D_MODEL, tn), lambda i, j: (0, j)),
            pl.BlockSpec((D_MODEL, LANES), lambda i, j: (0, 0)),
        ],
        out_specs=[
            pl.BlockSpec((tm, tn), lambda i, j: (i, j)),
            pl.BlockSpec((tm, LANES), lambda i, j: (i, 0)),
        ],
        out_shape=[
            jax.ShapeDtypeStruct((n, MAIN_W), bf16),
            jax.ShapeDtypeStruct((n, LANES), f32),
        ],
        scratch_shapes=[pltpu.VMEM((tm, D_MODEL), bf16)],
        compiler_params=_cparams("parallel", "arbitrary"),
        name="in_proj",
    )(x2d, gain, w_main, w_small)


def _rope_tables(t):
    half = ROPE_DIM // 2
    inv = ROPE_THETA ** (-jnp.arange(half, dtype=f32) * 2.0 / ROPE_DIM)
    ang = jnp.arange(t, dtype=f32)[:, None] * inv[None, :]
    cos, sin = jnp.cos(ang), jnp.sin(ang)
    pad = HEAD_DIM - ROPE_DIM
    c = jnp.concatenate([cos, cos, jnp.ones((t, pad), f32)], axis=1)
    s_lo = jnp.concatenate([-sin, jnp.zeros((t, half + pad), f32)], axis=1)
    s_hi = jnp.concatenate([jnp.zeros((t, half), f32), sin, jnp.zeros((t, pad), f32)], axis=1)
    rep = LANES // HEAD_DIM
    return jnp.tile(c, (1, rep)), jnp.tile(s_lo, (1, rep)), jnp.tile(s_hi, (1, rep))


def _rope(x, c, s_lo, s_hi):
    half = ROPE_DIM // 2
    return x * c + pltpu.roll(x, LANES - half, axis=1) * s_lo + pltpu.roll(x, half, axis=1) * s_hi


def _prep_kernel(q_ref, k_ref, iq_ref, sm_ref, c_ref, slo_ref, shi_ref, qg_ref, kg_ref, ikg_ref,
                 qo_ref, ko_ref, iqo_ref, iko_ref):
    c, s_lo, s_hi = c_ref[...], slo_ref[...], shi_ref[...]
    r = lax.broadcasted_iota(jnp.int32, (LANES, LANES), 0) // HEAD_DIM
    cc = lax.broadcasted_iota(jnp.int32, (LANES, LANES), 1) // HEAD_DIM
    gsum = jnp.where(r == cc, 1.0, 0.0).astype(bf16)

    def norm_rope(x, gain, scale):
        ms = _split_dot(x * x, gsum, 2) * (1.0 / HEAD_DIM)
        return _rope(x * gain, c, s_lo, s_hi) * (lax.rsqrt(ms + EPS) * scale)

    for t in range(Q_W // LANES):
        sl = slice(t * LANES, (t + 1) * LANES)
        qo_ref[:, sl] = norm_rope(q_ref[:, sl].astype(f32), qg_ref[...], HEAD_DIM ** -0.5).astype(bf16)
    for t in range(KV_W // LANES):
        sl = slice(t * LANES, (t + 1) * LANES)
        ko_ref[:, sl] = norm_rope(k_ref[:, sl].astype(f32), kg_ref[...], 1.0).astype(bf16)
    for t in range(IQ_W // LANES):
        sl = slice(t * LANES, (t + 1) * LANES)
        iqo_ref[:, sl] = (_rope(iq_ref[:, sl].astype(f32), c, s_lo, s_hi) * (IDX_DIM ** -0.5)).astype(bf16)
    ik = norm_rope(sm_ref[...], ikg_ref[...], 1.0)
    iko_ref[...] = ik[:, :IDX_DIM].astype(bf16)


def _prep(main, small, tables, q_gain, k_gain, ik_gain, t_len):
    n = main.shape[0]
    tp = min(512, t_len)
    npos = t_len // tp
    row = lambda w, idx: pl.BlockSpec((tp, w), lambda i, idx=idx: (i, idx))
    tab = pl.BlockSpec((tp, LANES), lambda i: (i % npos, 0))
    par = pl.BlockSpec((1, LANES), lambda i: (0, 0))
    return pl.pallas_call(
        _prep_kernel,
        grid=(n // tp,),
        in_specs=[row(Q_W, 0), row(KV_W, Q_W // KV_W), row(IQ_W, (Q_W + 2 * KV_W) // IQ_W), row(LANES, 0),
                  tab, tab, tab, par, par, par],
        out_specs=[row(Q_W, 0), row(KV_W, 0), row(IQ_W, 0), row(IDX_DIM, 0)],
        out_shape=[
            jax.ShapeDtypeStruct((n, Q_W), bf16),
            jax.ShapeDtypeStruct((n, KV_W), bf16),
            jax.ShapeDtypeStruct((n, IQ_W), bf16),
            jax.ShapeDtypeStruct((n, IDX_DIM), bf16),
        ],
        compiler_params=_cparams("parallel"),
        name="prep",
    )(main, main, main, small, *tables, q_gain, k_gain, ik_gain)


def _select_topk(key, adm, topk):
    rows, s = key.shape

    def count(m):
        return jnp.sum(jnp.where(m, 1.0, 0.0), axis=1, keepdims=True)

    t0 = jnp.where(count(key >= 0) >= topk, 0, INT_MIN).astype(jnp.int32)

    def vbit(i, t):
        cand = t + jnp.left_shift(jnp.int32(1), 30 - i)
        return jnp.where(count(key >= cand) >= topk, cand, t)

    thr = lax.fori_loop(0, 31, vbit, t0)
    above = key > thr
    ties = key == thr
    need = topk - count(above)
    idx = lax.broadcasted_iota(jnp.int32, (rows, s), 1)
    nbits = max(1, int(math.ceil(math.log2(s))))

    def ibit(i, m):
        cand = m + jnp.left_shift(jnp.int32(1), nbits - 1 - i)
        return jnp.where(count(ties & (idx < cand)) < need, cand, m)

    m = lax.fori_loop(0, nbits, ibit, jnp.zeros((rows, 1), jnp.int32))
    return (above | (ties & (idx <= m))) & adm


def _attn_kernel(q_ref, iq_ref, sm_ref, k_ref, v_ref, ik_ref, o_ref, *, s_len, blk, topk):
    iq = iq_ref[0]
    ik = ik_ref[0]
    iw = sm_ref[0][:, SM_IW:SM_IW + IDX_HEADS] * (IDX_HEADS ** -0.5)
    nt = (((1,), (1,)), ((), ()))
    score = jnp.zeros((Q_BLOCK, s_len), f32)
    for h in range(IDX_HEADS):
        rel = lax.dot_general(iq[:, h * IDX_DIM:(h + 1) * IDX_DIM], ik, nt, preferred_element_type=f32)
        score = score + jnp.maximum(rel, 0.0) * iw[:, h:h + 1]

    q_chunk = lax.broadcasted_iota(jnp.int32, (Q_BLOCK, s_len), 0) // CHUNK + blk * (Q_BLOCK // CHUNK)
    k_chunk = lax.broadcasted_iota(jnp.int32, (Q_BLOCK, s_len), 1) // CHUNK
    adm = k_chunk <= q_chunk
    if s_len <= topk:
        sel = adm
    else:
        bits = lax.bitcast_convert_type(score, jnp.int32)
        bits = jnp.where(bits == INT_MIN, 0, bits)
        key = jnp.where(bits >= 0, bits, bits ^ 0x7FFFFFFF)
        key = jnp.where(adm, key, INT_MIN)
        sel = _select_topk(key, adm, topk)
    bias = jnp.where(sel, 0.0, -jnp.inf)

    q = q_ref[0]
    k = k_ref[0]
    v = v_ref[0]
    grp = N_HEADS // N_KV_HEADS
    for g in range(N_KV_HEADS):
        kg = k[:, g * HEAD_DIM:(g + 1) * HEAD_DIM]
        vg = v[:, g * HEAD_DIM:(g + 1) * HEAD_DIM]
        for i in range(grp):
            h = g * grp + i
            logits = lax.dot_general(q[:, h * HEAD_DIM:(h + 1) * HEAD_DIM], kg, nt,
                                     preferred_element_type=f32) + bias
            mx = jnp.max(logits, axis=1, keepdims=True)
            p = jnp.exp(logits - mx)
            den = jnp.sum(p, axis=1, keepdims=True)
            o = jnp.dot(p.astype(bf16), vg, preferred_element_type=f32) / den
            o_ref[0, :, h * HEAD_DIM:(h + 1) * HEAD_DIM] = o.astype(bf16)


def _attention(qn, kn, iqr, ikn, main3, small3, topk):
    bsz, t_len, _ = qn.shape
    outs = []
    for blk in range(t_len // Q_BLOCK):
        s_len = (blk + 1) * Q_BLOCK
        qrow = lambda w, blk=blk: pl.BlockSpec((1, Q_BLOCK, w), lambda b: (b, blk, 0))
        krow = lambda w, col, s_len=s_len: pl.BlockSpec((1, s_len, w), lambda b, col=col: (b, 0, col))
        outs.append(pl.pallas_call(
            functools.partial(_attn_kernel, s_len=s_len, blk=blk, topk=topk),
            grid=(bsz,),
            in_specs=[qrow(Q_W), qrow(IQ_W), qrow(LANES), krow(KV_W, 0),
                      krow(KV_W, (Q_W + KV_W) // KV_W), krow(IDX_DIM, 0)],
            out_specs=pl.BlockSpec((1, Q_BLOCK, Q_W), lambda b: (b, 0, 0)),
            out_shape=jax.ShapeDtypeStruct((bsz, Q_BLOCK, Q_W), bf16),
            compiler_params=_cparams("parallel"),
            name=f"attn_{blk}",
        )(qn, iqr, small3, kn, main3, ikn))
    return jnp.concatenate(outs, axis=1)


def _ssd_kernel(xa_ref, xb_ref, xc_ref, z_ref, sm_ref, cw_ref, cb_ref, dtb_ref, alog_ref, dsk_ref, gn_ref,
                o_ref, state_ref, carry_ref, *, clen):
    @pl.when(pl.program_id(1) == 0)
    def _():
        state_ref[...] = jnp.zeros_like(state_ref)
        carry_ref[...] = jnp.zeros_like(carry_ref)

    raw = jnp.concatenate([xa_ref[0], xb_ref[0], xc_ref[0]], axis=1).astype(f32)
    prev = carry_ref[...]
    carry_ref[...] = raw[clen - 8:, :]
    row8 = lax.broadcasted_iota(jnp.int32, (8, SSD_XBC), 0)
    acc = raw * cw_ref[SSD_CONV - 1:SSD_CONV, :] + cb_ref[...]
    for s in range(1, SSD_CONV):
        r = pltpu.roll(raw, s, axis=0)
        head = jnp.where(row8 < s, pltpu.roll(prev, s, axis=0), r[:8])
        r = jnp.concatenate([head, r[8:]], axis=0)
        acc = acc + r * cw_ref[SSD_CONV - 1 - s:SSD_CONV - s, :]
    act = acc * jax.nn.sigmoid(acc)
    xs = act[:, :SSD_D_INNER]
    bm = act[:, SSD_D_INNER:SSD_D_INNER + SSD_BC].astype(bf16)
    cm = act[:, SSD_D_INNER + SSD_BC:].astype(bf16)

    dtr = sm_ref[0][:, SM_DT:SM_DT + SSD_HEADS] + dtb_ref[...]
    dt = jnp.maximum(dtr, 0.0) + jnp.log1p(jnp.exp(-jnp.abs(dtr)))
    a = dt * (-jnp.exp(alog_ref[...]))
    ri = lax.broadcasted_iota(jnp.int32, (clen, clen), 0)
    ci = lax.broadcasted_iota(jnp.int32, (clen, clen), 1)
    causal = ci <= ri
    tri = jnp.where(causal, 1.0, 0.0).astype(bf16)
    acs = None
    rem = a
    for _ in range(3):
        piece = rem.astype(bf16)
        d = jnp.dot(tri, piece, preferred_element_type=f32)
        acs = d if acs is None else acs + d
        rem = rem - piece.astype(f32)
    a_last = acs[clen - 1:clen, :]
    acs_t = jnp.concatenate([acs, jnp.zeros((clen, LANES - SSD_HEADS), f32)], axis=1).T

    hh = lax.broadcasted_iota(jnp.int32, (SSD_HEADS, SSD_D_INNER), 0)
    ch = lax.broadcasted_iota(jnp.int32, (SSD_HEADS, SSD_D_INNER), 1) // SSD_HEAD_DIM
    rep = jnp.where(hh == ch, 1.0, 0.0).astype(bf16)
    cd_rows = jnp.broadcast_to(jnp.exp(a_last), (8, SSD_HEADS))
    fac = _split_dot(jnp.concatenate([dt, jnp.exp(a_last - acs), jnp.exp(acs), cd_rows], axis=0), rep, 3)
    dt_e, dte_e, eacs_e, cd_e = fac[:clen], fac[clen:2 * clen], fac[2 * clen:3 * clen], fac[3 * clen:3 * clen + 1]

    xdt = xs * dt_e
    xdt_b = xdt.astype(bf16)
    xw_b = (xdt * dte_e).astype(bf16)
    z = z_ref[0].astype(f32)
    zg = z * jax.nn.sigmoid(z)
    gw = SSD_D_INNER // SSD_GROUPS
    hpg = SSD_HEADS // SSD_GROUPS
    for g in range(SSD_GROUPS):
        gs = slice(g * gw, (g + 1) * gw)
        bg = bm[:, g * SSD_STATE:(g + 1) * SSD_STATE]
        cg = cm[:, g * SSD_STATE:(g + 1) * SSD_STATE]
        cb = lax.dot_general(cg, bg, (((1,), (1,)), ((), ())), preferred_element_type=f32)
        hprev = state_ref[g]
        y_off = jnp.dot(cg, hprev.astype(bf16), preferred_element_type=f32)
        snew = lax.dot_general(bg, xw_b[:, gs], (((0,), (0,)), ((), ())), preferred_element_type=f32)
        state_ref[g] = cd_e[:, gs] * hprev + snew
        parts = []
        for r in range(hpg):
            h = g * hpg + r
            seg = acs[:, h:h + 1] - acs_t[h:h + 1, :]
            dec = jnp.exp(jnp.where(causal, seg, -jnp.inf))
            parts.append(jnp.dot((cb * dec).astype(bf16), xdt_b[:, h * SSD_HEAD_DIM:(h + 1) * SSD_HEAD_DIM],
                                 preferred_element_type=f32))
        y = jnp.concatenate(parts, axis=1) + y_off * eacs_e[:, gs] + dsk_ref[:, gs] * xs[:, gs]
        yz = y * zg[:, gs]
        ms = jnp.mean(yz * yz, axis=-1, keepdims=True)
        o_ref[0, :, gs] = (yz * lax.rsqrt(ms + EPS) * gn_ref[:, gs]).astype(bf16)


def _ssd(main3, small3, conv_w, conv_b, dt_bias, a_log, d_skip_e, ssd_norm):
    bsz, t_len, _ = main3.shape
    clen = min(SSD_CHUNK, t_len)
    xw = SSD_XBC // 3
    xcol = COL_XBC // xw
    blk = lambda w, col: pl.BlockSpec((1, clen, w), lambda b, c, col=col: (b, c, col))
    par = lambda r, w: pl.BlockSpec((r, w), lambda b, c: (0, 0))
    return pl.pallas_call(
        functools.partial(_ssd_kernel, clen=clen),
        grid=(bsz, t_len // clen),
        in_specs=[blk(xw, xcol), blk(xw, xcol + 1), blk(xw, xcol + 2), blk(SSD_D_INNER, COL_Z // SSD_D_INNER),
                  blk(LANES, 0), par(SSD_CONV, SSD_XBC), par(1, SSD_XBC), par(1, SSD_HEADS), par(1, SSD_HEADS),
                  par(1, SSD_D_INNER), par(1, SSD_D_INNER)],
        out_specs=pl.BlockSpec((1, clen, SSD_D_INNER), lambda b, c: (b, c, 0)),
        out_shape=jax.ShapeDtypeStruct((bsz, t_len, SSD_D_INNER), bf16),
        scratch_shapes=[pltpu.VMEM((SSD_GROUPS, SSD_STATE, SSD_D_INNER // SSD_GROUPS), f32),
                        pltpu.VMEM((8, SSD_XBC), f32)],
        compiler_params=_cparams("parallel", "arbitrary"),
        name="ssd",
    )(main3, main3, main3, main3, small3, conv_w, conv_b, dt_bias, a_log, d_skip_e, ssd_norm)


def _merge_kernel(ya_ref, ys_ref, ga_ref, gb_ref, x_ref, wa_ref, wb_ref, wo_ref, fg_ref, wr_ref, br_ref,
                  x2_ref, hn_ref, cmb_ref):
    a = jnp.dot(ya_ref[...], wa_ref[...], preferred_element_type=f32)
    s = jnp.dot(ys_ref[...], wb_ref[...], preferred_element_type=f32)
    merged = jax.nn.sigmoid(ga_ref[...].astype(f32)) * a + jax.nn.sigmoid(gb_ref[...].astype(f32)) * s
    x2 = x_ref[...] + jnp.dot(merged.astype(bf16), wo_ref[...], preferred_element_type=f32)
    x2_ref[...] = x2
    ms = jnp.mean(x2 * x2, axis=-1, keepdims=True)
    hn = x2 * lax.rsqrt(ms + EPS) * fg_ref[...]
    hn_ref[...] = hn.astype(bf16)

    logits = jnp.dot(hn, wr_ref[...], preferred_element_type=f32, precision=lax.Precision.HIGHEST) + br_ref[...]
    lane = lax.broadcasted_iota(jnp.int32, logits.shape, 1)
    neg = -jnp.inf
    big = jnp.int32(LANES)

    def first_max(mask):
        v = jnp.max(jnp.where(mask, logits, neg), axis=1, keepdims=True)
        i = jnp.min(jnp.where(mask & (logits == v), lane, big), axis=1, keepdims=True)
        return v, i

    is_g = lane < N_EXPERT_GROUPS
    gmax, gidx = first_max(is_g)
    g_w = 1.0 / jnp.sum(jnp.where(is_g, jnp.exp(logits - gmax), 0.0), axis=1, keepdims=True)
    e_lane = lane - N_EXPERT_GROUPS
    in_grp = (e_lane >= gidx * EXPERTS_PER_GROUP) & (e_lane < (gidx + 1) * EXPERTS_PER_GROUP)
    v1, i1 = first_max(in_grp)
    v2, i2 = first_max(in_grp & (lane != i1))
    e2 = jnp.exp(v2 - v1)
    p1 = 1.0 / (1.0 + e2)
    p2 = e2 / (1.0 + e2)
    cmb = jnp.where(lane == i1, g_w * p1, 0.0) + jnp.where(lane == i2, g_w * p2, 0.0)
    cmb_ref[...] = pltpu.roll(cmb, LANES - N_EXPERT_GROUPS, axis=1)


def _merge(y_attn, y_ssd, main, x2d, wa, wb, wo, ffn_gain, w_route, b_route):
    n = x2d.shape[0]
    tm = min(512, n)
    row = lambda w, col=0: pl.BlockSpec((tm, w), lambda i, col=col: (i, col))
    full = lambda r, w: pl.BlockSpec((r, w), lambda i: (0, 0))
    return pl.pallas_call(
        _merge_kernel,
        grid=(n // tm,),
        in_specs=[row(Q_W), row(SSD_D_INNER), row(D_MODEL, COL_GA // D_MODEL), row(D_MODEL, COL_GB // D_MODEL),
                  row(D_MODEL), full(Q_W, D_MODEL), full(SSD_D_INNER, D_MODEL), full(D_MODEL, D_MODEL),
                  full(1, D_MODEL), full(D_MODEL, LANES), full(1, LANES)],
        out_specs=[row(D_MODEL), row(D_MODEL), row(LANES)],
        out_shape=[
            jax.ShapeDtypeStruct((n, D_MODEL), f32),
            jax.ShapeDtypeStruct((n, D_MODEL), bf16),
            jax.ShapeDtypeStruct((n, LANES), f32),
        ],
        compiler_params=_cparams("parallel"),
        name="merge",
    )(y_attn, y_ssd, main, main, x2d, wa, wb, wo, ffn_gain, w_route, b_route)


def _moe_kernel(hn_ref, cmb_ref, x2_ref, wgu_ref, wd_ref, o_ref):
    e = pl.program_id(1)

    @pl.when(e == 0)
    def _():
        o_ref[...] = x2_ref[...]

    cmb = cmb_ref[...]
    lane = lax.broadcasted_iota(jnp.int32, cmb.shape, 1)
    scale = jnp.sum(jnp.where(lane == e, cmb, 0.0), axis=1, keepdims=True)
    au = jnp.dot(hn_ref[...], wgu_ref[0], preferred_element_type=f32)
    a = au[:, :EXPERT_HIDDEN]
    u = au[:, EXPERT_HIDDEN:]
    act = a * jax.nn.sigmoid(a) * u * scale
    o_ref[...] += jnp.dot(act.astype(bf16), wd_ref[0], preferred_element_type=f32)


def _moe(hn, cmb, x2, wgu, wd):
    n = hn.shape[0]
    tm = min(1024, n)
    return pl.pallas_call(
        _moe_kernel,
        grid=(n // tm, N_EXPERTS),
        in_specs=[
            pl.BlockSpec((tm, D_MODEL), lambda i, e: (i, 0)),
            pl.BlockSpec((tm, LANES), lambda i, e: (i, 0)),
            pl.BlockSpec((tm, D_MODEL), lambda i, e: (i, 0)),
            pl.BlockSpec((1, D_MODEL, 2 * EXPERT_HIDDEN), lambda i, e: (e, 0, 0)),
            pl.BlockSpec((1, EXPERT_HIDDEN, D_MODEL), lambda i, e: (e, 0, 0)),
        ],
        out_specs=pl.BlockSpec((tm, D_MODEL), lambda i, e: (i, 0)),
        out_shape=jax.ShapeDtypeStruct((n, D_MODEL), f32),
        compiler_params=_cparams("parallel", "arbitrary"),
        name="moe",
    )(hn, cmb, x2, wgu, wd)


def _layer(x, attn_norm, w_in, q_norm, k_norm, idx_k_norm, conv_w, conv_b, dt_bias, a_log, d_skip, ssd_norm,
           w_attn_branch, w_ssd_branch, w_out, ffn_norm, w_route_group, b_route_group, w_route_expert,
           b_route_expert, w_gate, w_up, w_down):
    bsz, t_len, _ = x.shape
    n = bsz * t_len
    offs = np.cumsum(SPLITS)[:-1].tolist()
    (wq, wk, wv, wiq, wik, wiw, wz, wxbc, wdt, wga, wgb) = jnp.split(w_in, offs, axis=-1)
    w_main = jnp.concatenate([wq, wk, wv, wiq, wz, wxbc, wga, wgb], axis=1).astype(bf16)
    pad = LANES - (IDX_DIM + IDX_HEADS + SSD_HEADS)
    w_small = jnp.concatenate([wik, wiw, wdt, jnp.zeros((D_MODEL, pad), f32)], axis=1).astype(bf16)

    x2d = x.reshape(n, D_MODEL)
    main, small = _in_proj(x2d, attn_norm.reshape(1, D_MODEL), w_main, w_small)

    rep = LANES // HEAD_DIM
    tables = _rope_tables(t_len)
    qn, kn, iqr, ikn = _prep(main, small, tables, jnp.tile(q_norm.reshape(1, HEAD_DIM), (1, rep)),
                             jnp.tile(k_norm.reshape(1, HEAD_DIM), (1, rep)),
                             jnp.tile(idx_k_norm.reshape(1, IDX_DIM), (1, rep)), t_len)

    main3 = main.reshape(bsz, t_len, MAIN_W)
    small3 = small.reshape(bsz, t_len, LANES)
    topk = min(TOPK_MAX, t_len // 4)
    y_attn = _attention(qn.reshape(bsz, t_len, Q_W), kn.reshape(bsz, t_len, KV_W),
                        iqr.reshape(bsz, t_len, IQ_W), ikn.reshape(bsz, t_len, IDX_DIM), main3, small3, topk)

    y_ssd = _ssd(main3, small3, conv_w, conv_b.reshape(1, SSD_XBC), dt_bias.reshape(1, SSD_HEADS),
                 a_log.reshape(1, SSD_HEADS), jnp.repeat(d_skip, SSD_HEAD_DIM).reshape(1, SSD_D_INNER),
                 ssd_norm.reshape(1, SSD_D_INNER))

    n_route = N_EXPERT_GROUPS + N_EXPERTS
    w_route = jnp.concatenate([w_route_group, w_route_expert, jnp.zeros((D_MODEL, LANES - n_route), f32)], axis=1)
    b_route = jnp.concatenate([b_route_group, b_route_expert, jnp.zeros((LANES - n_route,), f32)]).reshape(1, LANES)
    x2, hn, cmb = _merge(y_attn.reshape(n, Q_W), y_ssd.reshape(n, SSD_D_INNER), main, x2d,
                         w_attn_branch.astype(bf16), w_ssd_branch.astype(bf16), w_out.astype(bf16),
                         ffn_norm.reshape(1, D_MODEL), w_route, b_route)

    wgu = jnp.concatenate([w_gate, w_up], axis=-1).reshape(N_EXPERTS, D_MODEL, 2 * EXPERT_HIDDEN).astype(bf16)
    wd = w_down.reshape(N_EXPERTS, EXPERT_HIDDEN, D_MODEL).astype(bf16)
    out = _moe(hn, cmb, x2, wgu, wd)
    return out.reshape(bsz, t_len, D_MODEL)


def kernel(x, attn_norm, w_in, q_norm, k_norm, idx_k_norm, conv_w, conv_b, dt_bias, a_log, d_skip, ssd_norm,
           w_attn_branch, w_ssd_branch, w_out, ffn_norm, w_route_group, b_route_group, w_route_expert,
           b_route_expert, w_gate, w_up, w_down):
    for l in range(attn_norm.shape[0]):
        x = _layer(x, attn_norm[l], w_in[l], q_norm[l], k_norm[l], idx_k_norm[l], conv_w[l], conv_b[l], dt_bias[l],
                   a_log[l], d_skip[l], ssd_norm[l], w_attn_branch[l], w_ssd_branch[l], w_out[l], ffn_norm[l],
                   w_route_group[l], b_route_group[l], w_route_expert[l], b_route_expert[l], w_gate[l], w_up[l],
                   w_down[l])
    return x
```

```python
import functools
import math

import numpy as np
import jax
import jax.numpy as jnp
from jax import lax
from jax.experimental import pallas as pl
from jax.experimental.pallas import tpu as pltpu

f32 = jnp.float32
bf16 = jnp.bfloat16

D_MODEL = 1024
CHUNK = 64
Q_BLOCK = 128
EPS = 1e-6
N_HEADS = 16
N_KV_HEADS = 4
HEAD_DIM = 64
ROPE_DIM = HEAD_DIM // 4
ROPE_THETA = 500000.0
IDX_HEADS = 8
IDX_DIM = 64
TOPK_MAX = 256
SSD_D_INNER = 2 * D_MODEL
SSD_HEAD_DIM = 64
SSD_HEADS = SSD_D_INNER // SSD_HEAD_DIM
SSD_GROUPS = 4
SSD_STATE = 128
SSD_CONV = 4
SSD_BC = SSD_GROUPS * SSD_STATE
SSD_XBC = SSD_D_INNER + 2 * SSD_BC
N_EXPERT_GROUPS = 4
EXPERTS_PER_GROUP = 8
N_EXPERTS = N_EXPERT_GROUPS * EXPERTS_PER_GROUP
EXPERT_HIDDEN = 256
Q_W = N_HEADS * HEAD_DIM
KV_W = N_KV_HEADS * HEAD_DIM
IQ_W = IDX_HEADS * IDX_DIM
SPLITS = (Q_W, KV_W, KV_W, IQ_W, IDX_DIM, IDX_HEADS, SSD_D_INNER, SSD_XBC, SSD_HEADS, D_MODEL, D_MODEL)

LANES = 128
MAIN_W = Q_W + 2 * KV_W + IQ_W + SSD_D_INNER + SSD_XBC + 2 * D_MODEL
COL_Z = Q_W + 2 * KV_W + IQ_W
COL_XBC = COL_Z + SSD_D_INNER
COL_GA = COL_XBC + SSD_XBC
COL_GB = COL_GA + D_MODEL
SM_IW = IDX_DIM
SM_DT = IDX_DIM + IDX_HEADS
SSD_CHUNK = 128
VMEM_LIMIT = 56 * 1024 * 1024
INT_MIN = -(2 ** 31)


def _cparams(*sem):
    return pltpu.CompilerParams(dimension_semantics=sem, vmem_limit_bytes=VMEM_LIMIT)


def _split_dot(a, b, parts):
    acc = None
    rem = a
    for _ in range(parts):
        piece = rem.astype(bf16)
        d = jnp.dot(piece, b, preferred_element_type=f32)
        acc = d if acc is None else acc + d
        rem = rem - piece.astype(f32)
    return acc


def _inproj_kernel(x_ref, g_ref, w_ref, ws_ref, o_ref, os_ref, h_ref):
    @pl.when(pl.program_id(1) == 0)
    def _():
        x = x_ref[...]
        ms = jnp.mean(x * x, axis=-1, keepdims=True)
        hb = (x * lax.rsqrt(ms + EPS) * g_ref[...]).astype(bf16)
        h_ref[...] = hb
        os_ref[...] = jnp.dot(hb, ws_ref[...], preferred_element_type=f32)

    o_ref[...] = jnp.dot(h_ref[...], w_ref[...], preferred_element_type=f32).astype(bf16)


def _in_proj(x2d, gain, w_main, w_small):
    n = x2d.shape[0]
    tm = min(1024, n)
    tn = 1024
    return pl.pallas_call(
        _inproj_kernel,
        grid=(n // tm, MAIN_W // tn),
        in_specs=[
            pl.BlockSpec((tm, D_MODEL), lambda i, j: (i, 0)),
            pl.BlockSpec((1, D_MODEL), lambda i, j: (0, 0)),
            pl.BlockSpec((D_MODEL, tn), lambda i, j: (0, j)),
            pl.BlockSpec((D_MODEL, LANES), lambda i, j: (0, 0)),
        ],
        out_specs=[
            pl.BlockSpec((tm, tn), lambda i, j: (i, j)),
            pl.BlockSpec((tm, LANES), lambda i, j: (i, 0)),
        ],
        out_shape=[
            jax.ShapeDtypeStruct((n, MAIN_W), bf16),
            jax.ShapeDtypeStruct((n, LANES), f32),
        ],
        scratch_shapes=[pltpu.VMEM((tm, D_MODEL), bf16)],
        compiler_params=_cparams("parallel", "arbitrary"),
        name="in_proj",
    )(x2d, gain, w_main, w_small)


def _rope_tables(t):
    half = ROPE_DIM // 2
    inv = ROPE_THETA ** (-jnp.arange(half, dtype=f32) * 2.0 / ROPE_DIM)
    ang = jnp.arange(t, dtype=f32)[:, None] * inv[None, :]
    cos, sin = jnp.cos(ang), jnp.sin(ang)
    pad = HEAD_DIM - ROPE_DIM
    c = jnp.concatenate([cos, cos, jnp.ones((t, pad), f32)], axis=1)
    s_lo = jnp.concatenate([-sin, jnp.zeros((t, half + pad), f32)], axis=1)
    s_hi = jnp.concatenate([jnp.zeros((t, half), f32), sin, jnp.zeros((t, pad), f32)], axis=1)
    rep = LANES // HEAD_DIM
    return jnp.tile(c, (1, rep)), jnp.tile(s_lo, (1, rep)), jnp.tile(s_hi, (1, rep))


def _rope(x, c, s_lo, s_hi):
    half = ROPE_DIM // 2
    return x * c + pltpu.roll(x, LANES - half, axis=1) * s_lo + pltpu.roll(x, half, axis=1) * s_hi


def _prep_kernel(q_ref, k_ref, v_ref, iq_ref, sm_ref, c_ref, slo_ref, shi_ref, qg_ref, kg_ref, ikg_ref,
                 qo_ref, ko_ref, vo_ref, iqo_ref, iko_ref):
    c, s_lo, s_hi = c_ref[...], slo_ref[...], shi_ref[...]
    r = lax.broadcasted_iota(jnp.int32, (LANES, LANES), 0) // HEAD_DIM
    cc = lax.broadcasted_iota(jnp.int32, (LANES, LANES), 1) // HEAD_DIM
    gsum = jnp.where(r == cc, 1.0, 0.0).astype(bf16)

    def norm_rope(x, gain, scale):
        ms = _split_dot(x * x, gsum, 2) * (1.0 / HEAD_DIM)
        return _rope(x * gain, c, s_lo, s_hi) * (lax.rsqrt(ms + EPS) * scale)

    def store_heads(o_ref, t, y):
        yb = y.astype(bf16)
        o_ref[0, 2 * t] = yb[:, :HEAD_DIM]
        o_ref[0, 2 * t + 1] = yb[:, HEAD_DIM:]

    for t in range(Q_W // LANES):
        sl = slice(t * LANES, (t + 1) * LANES)
        store_heads(qo_ref, t, norm_rope(q_ref[0, :, sl].astype(f32), qg_ref[...], HEAD_DIM ** -0.5))
    for t in range(KV_W // LANES):
        sl = slice(t * LANES, (t + 1) * LANES)
        store_heads(ko_ref, t, norm_rope(k_ref[0, :, sl].astype(f32), kg_ref[...], 1.0))
        store_heads(vo_ref, t, v_ref[0, :, sl])
    for t in range(IQ_W // LANES):
        sl = slice(t * LANES, (t + 1) * LANES)
        store_heads(iqo_ref, t, _rope(iq_ref[0, :, sl].astype(f32), c, s_lo, s_hi) * (IDX_DIM ** -0.5))
    ik = norm_rope(sm_ref[0], ikg_ref[...], 1.0)
    iko_ref[0] = ik[:, :IDX_DIM].astype(bf16)


def _prep(main3, small3, tables, q_gain, k_gain, ik_gain):
    bsz, t_len, _ = main3.shape
    tp = min(512, t_len)
    row = lambda w, col: pl.BlockSpec((1, tp, w), lambda b, i, col=col: (b, i, col))
    heads = lambda nh: pl.BlockSpec((1, nh, tp, HEAD_DIM), lambda b, i: (b, 0, i, 0))
    tab = pl.BlockSpec((tp, LANES), lambda b, i: (i, 0))
    par = pl.BlockSpec((1, LANES), lambda b, i: (0, 0))
    hshape = lambda nh: jax.ShapeDtypeStruct((bsz, nh, t_len, HEAD_DIM), bf16)
    return pl.pallas_call(
        _prep_kernel,
        grid=(bsz, t_len // tp),
        in_specs=[row(Q_W, 0), row(KV_W, Q_W // KV_W), row(KV_W, (Q_W + KV_W) // KV_W),
                  row(IQ_W, (Q_W + 2 * KV_W) // IQ_W), row(LANES, 0), tab, tab, tab, par, par, par],
        out_specs=[heads(N_HEADS), heads(N_KV_HEADS), heads(N_KV_HEADS), heads(IDX_HEADS),
                   pl.BlockSpec((1, tp, IDX_DIM), lambda b, i: (b, i, 0))],
        out_shape=[hshape(N_HEADS), hshape(N_KV_HEADS), hshape(N_KV_HEADS), hshape(IDX_HEADS),
                   jax.ShapeDtypeStruct((bsz, t_len, IDX_DIM), bf16)],
        compiler_params=_cparams("parallel", "parallel"),
        name="prep",
    )(main3, main3, main3, main3, small3, *tables, q_gain, k_gain, ik_gain)


def _select_topk(score, topk):
    s, cols = score.shape

    def count(m):
        ones = jnp.where(m, 1.0, 0.0).reshape(s // Q_BLOCK, Q_BLOCK, cols)
        return jnp.sum(jnp.sum(ones, axis=0), axis=0, keepdims=True)

    def as_float(image):
        return lax.bitcast_convert_type(jnp.where(image >= 0, image, image ^ 0x7FFFFFFF), f32)

    t0 = jnp.where(count(score >= 0.0) >= topk, 0, INT_MIN).astype(jnp.int32)

    def vbit(i, t):
        cand = t + jnp.left_shift(jnp.int32(1), 30 - i)
        return jnp.where(count(score >= as_float(cand)) >= topk, cand, t)

    thr = as_float(lax.fori_loop(0, 31, vbit, t0))
    above = score > thr
    ties = score == thr
    need = topk - count(above)
    idx = lax.broadcasted_iota(jnp.int32, (s, cols), 0)
    nbits = max(1, int(math.ceil(math.log2(s))))

    def ibit(i, m):
        cand = m + jnp.left_shift(jnp.int32(1), nbits - 1 - i)
        return jnp.where(count(ties & (idx < cand)) < need, cand, m)

    def tie_break():
        return lax.fori_loop(0, nbits, ibit, jnp.zeros((1, cols), jnp.int32))

    surplus = jnp.max(jnp.where(count(ties) > need, 1.0, 0.0)) > 0.0
    m = lax.cond(surplus, tie_break, lambda: jnp.full((1, cols), s, jnp.int32))
    return above | (ties & (idx <= m))


def _attn_kernel(q_ref, iq_ref, sm_ref, k_ref, v_ref, ik_ref, *rest, s_len, blk, topk):
    o_ref = rest[-1]
    nt = (((1,), (1,)), ((), ()))
    ik = ik_ref[0]
    iw_t = sm_ref[0].T[SM_IW:SM_IW + IDX_HEADS, :] * (IDX_HEADS ** -0.5)
    score = jnp.zeros((s_len, Q_BLOCK), f32)
    for hp in range(IDX_HEADS // 2):
        iq2 = iq_ref[0, 2 * hp:2 * hp + 2].reshape(2 * Q_BLOCK, IDX_DIM)
        rel = lax.dot_general(ik, iq2, nt, preferred_element_type=f32)
        score = (score + jnp.maximum(rel[:, :Q_BLOCK], 0.0) * iw_t[2 * hp:2 * hp + 1]
                 + jnp.maximum(rel[:, Q_BLOCK:], 0.0) * iw_t[2 * hp + 1:2 * hp + 2])

    k_chunk = lax.broadcasted_iota(jnp.int32, (s_len, Q_BLOCK), 0) // CHUNK
    q_chunk = lax.broadcasted_iota(jnp.int32, (s_len, Q_BLOCK), 1) // CHUNK + blk * (Q_BLOCK // CHUNK)
    adm = k_chunk <= q_chunk
    if s_len <= topk:
        sel = adm
    else:
        sel = _select_topk(jnp.where(adm, score, -jnp.inf), topk)
    bias = jnp.where(sel, 0.0, -jnp.inf).T

    grp = N_HEADS // N_KV_HEADS
    for g in range(N_KV_HEADS):
        qg = q_ref[0, g * grp:(g + 1) * grp].reshape(grp * Q_BLOCK, HEAD_DIM)
        logits = lax.dot_general(qg, k_ref[0, g], nt, preferred_element_type=f32)
        logits = logits.reshape(grp, Q_BLOCK, s_len) + bias[None]
        mx = jnp.max(logits, axis=2, keepdims=True)
        p = jnp.exp(logits - mx)
        den = jnp.sum(p, axis=2, keepdims=True)
        o = jnp.dot(p.reshape(grp * Q_BLOCK, s_len).astype(bf16), v_ref[0, g], preferred_element_type=f32)
        o = o.reshape(grp, Q_BLOCK, HEAD_DIM) / den
        for i in range(grp):
            h = g * grp + i
            o_ref[0, :, h * HEAD_DIM:(h + 1) * HEAD_DIM] = o[i].astype(bf16)


def _attention(qh, kh, vh, iqh, ikn, small3, topk):
    bsz, _, t_len, _ = qh.shape
    assert topk % CHUNK == 0
    y = None
    for blk in range(t_len // Q_BLOCK):
        s_len = (blk + 1) * Q_BLOCK
        qhead = lambda nh, blk=blk: pl.BlockSpec((1, nh, Q_BLOCK, HEAD_DIM), lambda b: (b, 0, blk, 0))
        khead = lambda nh, s_len=s_len: pl.BlockSpec((1, nh, s_len, HEAD_DIM), lambda b: (b, 0, 0, 0))
        in_specs = [qhead(N_HEADS), qhead(IDX_HEADS), pl.BlockSpec((1, Q_BLOCK, LANES), lambda b, blk=blk: (b, blk, 0)),
                    khead(N_KV_HEADS), khead(N_KV_HEADS), pl.BlockSpec((1, s_len, IDX_DIM), lambda b: (b, 0, 0))]
        args = [qh, iqh, small3, kh, vh, ikn]
        aliases = {}
        if y is not None:
            in_specs.append(pl.BlockSpec(memory_space=pl.ANY))
            args.append(y)
            aliases = {len(args) - 1: 0}
        y = pl.pallas_call(
            functools.partial(_attn_kernel, s_len=s_len, blk=blk, topk=topk),
            grid=(bsz,),
            in_specs=in_specs,
            out_specs=pl.BlockSpec((1, Q_BLOCK, Q_W), lambda b, blk=blk: (b, blk, 0)),
            out_shape=jax.ShapeDtypeStruct((bsz, t_len, Q_W), bf16),
            input_output_aliases=aliases,
            compiler_params=_cparams("parallel"),
            name=f"attn_{blk}",
        )(*args)
    return y


def _ssd_kernel(xa_ref, xb_ref, xc_ref, z_ref, sm_ref, cw_ref, cb_ref, dtb_ref, alog_ref, dsk_ref, gn_ref,
                o_ref, state_ref, carry_ref, *, clen):
    @pl.when(pl.program_id(1) == 0)
    def _():
        state_ref[...] = jnp.zeros_like(state_ref)
        carry_ref[...] = jnp.zeros_like(carry_ref)

    raw = jnp.concatenate([xa_ref[0], xb_ref[0], xc_ref[0]], axis=1).astype(f32)
    prev = carry_ref[...]
    carry_ref[...] = raw[clen - 8:, :]
    row8 = lax.broadcasted_iota(jnp.int32, (8, SSD_XBC), 0)
    acc = raw * cw_ref[SSD_CONV - 1:SSD_CONV, :] + cb_ref[...]
    for s in range(1, SSD_CONV):
        r = pltpu.roll(raw, s, axis=0)
        head = jnp.where(row8 < s, pltpu.roll(prev, s, axis=0), r[:8])
        r = jnp.concatenate([head, r[8:]], axis=0)
        acc = acc + r * cw_ref[SSD_CONV - 1 - s:SSD_CONV - s, :]
    act = acc * jax.nn.sigmoid(acc)
    xs = act[:, :SSD_D_INNER]
    bm = act[:, SSD_D_INNER:SSD_D_INNER + SSD_BC].astype(bf16)
    cm = act[:, SSD_D_INNER + SSD_BC:].astype(bf16)

    dtr = sm_ref[0][:, SM_DT:SM_DT + SSD_HEADS] + dtb_ref[...]
    dt = jnp.maximum(dtr, 0.0) + jnp.log1p(jnp.exp(-jnp.abs(dtr)))
    a = dt * (-jnp.exp(alog_ref[...]))
    ri = lax.broadcasted_iota(jnp.int32, (clen, clen), 0)
    ci = lax.broadcasted_iota(jnp.int32, (clen, clen), 1)
    causal = ci <= ri
    tri = jnp.where(causal, 1.0, 0.0).astype(bf16)
    acs = None
    rem = a
    for _ in range(3):
        piece = rem.astype(bf16)
        d = jnp.dot(tri, piece, preferred_element_type=f32)
        acs = d if acs is None else acs + d
        rem = rem - piece.astype(f32)
    a_last = acs[clen - 1:clen, :]
    acs_t = jnp.concatenate([acs, jnp.zeros((clen, LANES - SSD_HEADS), f32)], axis=1).T

    hh = lax.broadcasted_iota(jnp.int32, (SSD_HEADS, SSD_D_INNER), 0)
    ch = lax.broadcasted_iota(jnp.int32, (SSD_HEADS, SSD_D_INNER), 1) // SSD_HEAD_DIM
    rep = jnp.where(hh == ch, 1.0, 0.0).astype(bf16)
    cd_rows = jnp.broadcast_to(jnp.exp(a_last), (8, SSD_HEADS))
    fac = _split_dot(jnp.concatenate([dt, jnp.exp(a_last - acs), jnp.exp(acs), cd_rows], axis=0), rep, 2)
    dt_e, dte_e, eacs_e, cd_e = fac[:clen], fac[clen:2 * clen], fac[2 * clen:3 * clen], fac[3 * clen:3 * clen + 1]

    xdt = xs * dt_e
    xdt_b = xdt.astype(bf16)
    xw_b = (xdt * dte_e).astype(bf16)
    z = z_ref[0].astype(f32)
    zg = z * jax.nn.sigmoid(z)
    gw = SSD_D_INNER // SSD_GROUPS
    hpg = SSD_HEADS // SSD_GROUPS
    for g in range(SSD_GROUPS):
        gs = slice(g * gw, (g + 1) * gw)
        bg = bm[:, g * SSD_STATE:(g + 1) * SSD_STATE]
        cg = cm[:, g * SSD_STATE:(g + 1) * SSD_STATE]
        cb = lax.dot_general(cg, bg, (((1,), (1,)), ((), ())), preferred_element_type=f32)
        hprev = state_ref[g]
        y_off = jnp.dot(cg, hprev.astype(bf16), preferred_element_type=f32)
        snew = lax.dot_general(bg, xw_b[:, gs], (((0,), (0,)), ((), ())), preferred_element_type=f32)
        state_ref[g] = cd_e[:, gs] * hprev + snew
        parts = []
        for r in range(hpg):
            h = g * hpg + r
            seg = acs[:, h:h + 1] - acs_t[h:h + 1, :]
            dec = jnp.exp(jnp.where(causal, seg, -jnp.inf))
            parts.append(jnp.dot((cb * dec).astype(bf16), xdt_b[:, h * SSD_HEAD_DIM:(h + 1) * SSD_HEAD_DIM],
                                 preferred_element_type=f32))
        y = jnp.concatenate(parts, axis=1) + y_off * eacs_e[:, gs] + dsk_ref[:, gs] * xs[:, gs]
        yz = y * zg[:, gs]
        ms = jnp.mean(yz * yz, axis=-1, keepdims=True)
        o_ref[0, :, gs] = (yz * lax.rsqrt(ms + EPS) * gn_ref[:, gs]).astype(bf16)


def _ssd(main3, small3, conv_w, conv_b, dt_bias, a_log, d_skip_e, ssd_norm):
    bsz, t_len, _ = main3.shape
    clen = min(SSD_CHUNK, t_len)
    xw = SSD_XBC // 3
    xcol = COL_XBC // xw
    blk = lambda w, col: pl.BlockSpec((1, clen, w), lambda b, c, col=col: (b, c, col))
    par = lambda r, w: pl.BlockSpec((r, w), lambda b, c: (0, 0))
    return pl.pallas_call(
        functools.partial(_ssd_kernel, clen=clen),
        grid=(bsz, t_len // clen),
        in_specs=[blk(xw, xcol), blk(xw, xcol + 1), blk(xw, xcol + 2), blk(SSD_D_INNER, COL_Z // SSD_D_INNER),
                  blk(LANES, 0), par(SSD_CONV, SSD_XBC), par(1, SSD_XBC), par(1, SSD_HEADS), par(1, SSD_HEADS),
                  par(1, SSD_D_INNER), par(1, SSD_D_INNER)],
        out_specs=pl.BlockSpec((1, clen, SSD_D_INNER), lambda b, c: (b, c, 0)),
        out_shape=jax.ShapeDtypeStruct((bsz, t_len, SSD_D_INNER), bf16),
        scratch_shapes=[pltpu.VMEM((SSD_GROUPS, SSD_STATE, SSD_D_INNER // SSD_GROUPS), f32),
                        pltpu.VMEM((8, SSD_XBC), f32)],
        compiler_params=_cparams("parallel", "arbitrary"),
        name="ssd",
    )(main3, main3, main3, main3, small3, conv_w, conv_b, dt_bias, a_log, d_skip_e, ssd_norm)


def _merge_kernel(ya_ref, ys_ref, ga_ref, gb_ref, x_ref, wa_ref, wb_ref, wo_ref, fg_ref, wr_ref, br_ref,
                  x2_ref, hn_ref, cmb_ref):
    a = jnp.dot(ya_ref[...], wa_ref[...], preferred_element_type=f32)
    s = jnp.dot(ys_ref[...], wb_ref[...], preferred_element_type=f32)
    merged = jax.nn.sigmoid(ga_ref[...].astype(f32)) * a + jax.nn.sigmoid(gb_ref[...].astype(f32)) * s
    x2 = x_ref[...] + jnp.dot(merged.astype(bf16), wo_ref[...], preferred_element_type=f32)
    x2_ref[...] = x2
    ms = jnp.mean(x2 * x2, axis=-1, keepdims=True)
    hn = x2 * lax.rsqrt(ms + EPS) * fg_ref[...]
    hn_hi = hn.astype(bf16)
    hn_ref[...] = hn_hi
    hn_lo = (hn - hn_hi.astype(f32)).astype(bf16)
    hw = jnp.dot(hn_hi, wr_ref[...], preferred_element_type=f32)
    logits = (hw[:, :LANES] + hw[:, LANES:] + jnp.dot(hn_lo, wr_ref[:, :LANES], preferred_element_type=f32)
              + br_ref[...])
    lane = lax.broadcasted_iota(jnp.int32, logits.shape, 1)
    neg = -jnp.inf
    big = jnp.int32(LANES)

    def first_max(mask):
        v = jnp.max(jnp.where(mask, logits, neg), axis=1, keepdims=True)
        i = jnp.min(jnp.where(mask & (logits == v), lane, big), axis=1, keepdims=True)
        return v, i

    is_g = lane < N_EXPERT_GROUPS
    gmax, gidx = first_max(is_g)
    g_w = 1.0 / jnp.sum(jnp.where(is_g, jnp.exp(logits - gmax), 0.0), axis=1, keepdims=True)
    e_lane = lane - N_EXPERT_GROUPS
    in_grp = (e_lane >= gidx * EXPERTS_PER_GROUP) & (e_lane < (gidx + 1) * EXPERTS_PER_GROUP)
    v1, i1 = first_max(in_grp)
    v2, i2 = first_max(in_grp & (lane != i1))
    e2 = jnp.exp(v2 - v1)
    p1 = 1.0 / (1.0 + e2)
    p2 = e2 / (1.0 + e2)
    cmb = jnp.where(lane == i1, g_w * p1, 0.0) + jnp.where(lane == i2, g_w * p2, 0.0)
    cmb_ref[...] = pltpu.roll(cmb, LANES - N_EXPERT_GROUPS, axis=1)


def _merge(y_attn, y_ssd, main, x2d, wa, wb, wo, ffn_gain, w_route, b_route):
    n = x2d.shape[0]
    tm = min(512, n)
    row = lambda w, col=0: pl.BlockSpec((tm, w), lambda i, col=col: (i, col))
    full = lambda r, w: pl.BlockSpec((r, w), lambda i: (0, 0))
    return pl.pallas_call(
        _merge_kernel,
        grid=(n // tm,),
        in_specs=[row(Q_W), row(SSD_D_INNER), row(D_MODEL, COL_GA // D_MODEL), row(D_MODEL, COL_GB // D_MODEL),
                  row(D_MODEL), full(Q_W, D_MODEL), full(SSD_D_INNER, D_MODEL), full(D_MODEL, D_MODEL),
                  full(1, D_MODEL), full(D_MODEL, 2 * LANES), full(1, LANES)],
        out_specs=[row(D_MODEL), row(D_MODEL), row(LANES)],
        out_shape=[
            jax.ShapeDtypeStruct((n, D_MODEL), f32),
            jax.ShapeDtypeStruct((n, D_MODEL), bf16),
            jax.ShapeDtypeStruct((n, LANES), f32),
        ],
        compiler_params=_cparams("parallel"),
        name="merge",
    )(y_attn, y_ssd, main, main, x2d, wa, wb, wo, ffn_gain, w_route, b_route)


def _moe_kernel(hn_ref, cmb_ref, x2_ref, wgu_ref, wd_ref, o_ref):
    e = pl.program_id(1)

    @pl.when(e == 0)
    def _():
        o_ref[...] = x2_ref[...]

    cmb = cmb_ref[...]
    lane = lax.broadcasted_iota(jnp.int32, cmb.shape, 1)
    scale = jnp.sum(jnp.where(lane == e, cmb, 0.0), axis=1, keepdims=True)
    au = jnp.dot(hn_ref[...], wgu_ref[0], preferred_element_type=f32)
    a = au[:, :EXPERT_HIDDEN]
    u = au[:, EXPERT_HIDDEN:]
    act = a * jax.nn.sigmoid(a) * u * scale
    o_ref[...] += jnp.dot(act.astype(bf16), wd_ref[0], preferred_element_type=f32)


def _moe(hn, cmb, x2, wgu, wd):
    n = hn.shape[0]
    tm = min(1024, n)
    return pl.pallas_call(
        _moe_kernel,
        grid=(n // tm, N_EXPERTS),
        in_specs=[
            pl.BlockSpec((tm, D_MODEL), lambda i, e: (i, 0)),
            pl.BlockSpec((tm, LANES), lambda i, e: (i, 0)),
            pl.BlockSpec((tm, D_MODEL), lambda i, e: (i, 0)),
            pl.BlockSpec((1, D_MODEL, 2 * EXPERT_HIDDEN), lambda i, e: (e, 0, 0)),
            pl.BlockSpec((1, EXPERT_HIDDEN, D_MODEL), lambda i, e: (e, 0, 0)),
        ],
        out_specs=pl.BlockSpec((tm, D_MODEL), lambda i, e: (i, 0)),
        out_shape=jax.ShapeDtypeStruct((n, D_MODEL), f32),
        compiler_params=_cparams("parallel", "arbitrary"),
        name="moe",
    )(hn, cmb, x2, wgu, wd)


def _layer(x, attn_norm, w_in, q_norm, k_norm, idx_k_norm, conv_w, conv_b, dt_bias, a_log, d_skip, ssd_norm,
           w_attn_branch, w_ssd_branch, w_out, ffn_norm, w_route_group, b_route_group, w_route_expert,
           b_route_expert, w_gate, w_up, w_down):
    bsz, t_len, _ = x.shape
    n = bsz * t_len
    offs = np.cumsum(SPLITS)[:-1].tolist()
    (wq, wk, wv, wiq, wik, wiw, wz, wxbc, wdt, wga, wgb) = jnp.split(w_in, offs, axis=-1)
    w_main = jnp.concatenate([wq, wk, wv, wiq, wz, wxbc, wga, wgb], axis=1).astype(bf16)
    pad = LANES - (IDX_DIM + IDX_HEADS + SSD_HEADS)
    w_small = jnp.concatenate([wik, wiw, wdt, jnp.zeros((D_MODEL, pad), f32)], axis=1).astype(bf16)

    x2d = x.reshape(n, D_MODEL)
    main, small = _in_proj(x2d, attn_norm.reshape(1, D_MODEL), w_main, w_small)

    main3 = main.reshape(bsz, t_len, MAIN_W)
    small3 = small.reshape(bsz, t_len, LANES)
    rep = LANES // HEAD_DIM
    qh, kh, vh, iqh, ikn = _prep(main3, small3, _rope_tables(t_len),
                                 jnp.tile(q_norm.reshape(1, HEAD_DIM), (1, rep)),
                                 jnp.tile(k_norm.reshape(1, HEAD_DIM), (1, rep)),
                                 jnp.tile(idx_k_norm.reshape(1, IDX_DIM), (1, rep)))
    y_attn = _attention(qh, kh, vh, iqh, ikn, small3, min(TOPK_MAX, t_len // 4))

    y_ssd = _ssd(main3, small3, conv_w, conv_b.reshape(1, SSD_XBC), dt_bias.reshape(1, SSD_HEADS),
                 a_log.reshape(1, SSD_HEADS), jnp.repeat(d_skip, SSD_HEAD_DIM).reshape(1, SSD_D_INNER),
                 ssd_norm.reshape(1, SSD_D_INNER))

    n_route = N_EXPERT_GROUPS + N_EXPERTS
    w_route = jnp.concatenate([w_route_group, w_route_expert, jnp.zeros((D_MODEL, LANES - n_route), f32)], axis=1)
    w_route_hi = w_route.astype(bf16)
    w_route = jnp.concatenate([w_route_hi, (w_route - w_route_hi.astype(f32)).astype(bf16)], axis=1)
    b_route = jnp.concatenate([b_route_group, b_route_expert, jnp.zeros((LANES - n_route,), f32)]).reshape(1, LANES)
    x2, hn, cmb = _merge(y_attn.reshape(n, Q_W), y_ssd.reshape(n, SSD_D_INNER), main, x2d,
                         w_attn_branch.astype(bf16), w_ssd_branch.astype(bf16), w_out.astype(bf16),
                         ffn_norm.reshape(1, D_MODEL), w_route, b_route)

    wgu = jnp.concatenate([w_gate, w_up], axis=-1).reshape(N_EXPERTS, D_MODEL, 2 * EXPERT_HIDDEN).astype(bf16)
    wd = w_down.reshape(N_EXPERTS, EXPERT_HIDDEN, D_MODEL).astype(bf16)
    out = _moe(hn, cmb, x2, wgu, wd)
    return out.reshape(bsz, t_len, D_MODEL)


def kernel(x, attn_norm, w_in, q_norm, k_norm, idx_k_norm, conv_w, conv_b, dt_bias, a_log, d_skip, ssd_norm,
           w_attn_branch, w_ssd_branch, w_out, ffn_norm, w_route_group, b_route_group, w_route_expert,
           b_route_expert, w_gate, w_up, w_down):
    for l in range(attn_norm.shape[0]):
        x = _layer(x, attn_norm[l], w_in[l], q_norm[l], k_norm[l], idx_k_norm[l], conv_w[l], conv_b[l], dt_bias[l],
                   a_log[l], d_skip[l], ssd_norm[l], w_attn_branch[l], w_ssd_branch[l], w_out[l], ffn_norm[l],
                   w_route_group[l], b_route_group[l], w_route_expert[l], b_route_expert[l], w_gate[l], w_up[l],
                   w_down[l])
    return x
```

```python
import functools
import math

import numpy as np
import jax
import jax.numpy as jnp
from jax import lax
from jax.experimental import pallas as pl
from jax.experimental.pallas import tpu as pltpu

f32 = jnp.float32
bf16 = jnp.bfloat16

D_MODEL = 1024
CHUNK = 64
Q_BLOCK = 128
EPS = 1e-6
N_HEADS = 16
N_KV_HEADS = 4
HEAD_DIM = 64
ROPE_DIM = HEAD_DIM // 4
ROPE_THETA = 500000.0
IDX_HEADS = 8
IDX_DIM = 64
TOPK_MAX = 256
SSD_D_INNER = 2 * D_MODEL
SSD_HEAD_DIM = 64
SSD_HEADS = SSD_D_INNER // SSD_HEAD_DIM
SSD_GROUPS = 4
SSD_STATE = 128
SSD_CONV = 4
SSD_BC = SSD_GROUPS * SSD_STATE
SSD_XBC = SSD_D_INNER + 2 * SSD_BC
N_EXPERT_GROUPS = 4
EXPERTS_PER_GROUP = 8
N_EXPERTS = N_EXPERT_GROUPS * EXPERTS_PER_GROUP
EXPERT_HIDDEN = 256
Q_W = N_HEADS * HEAD_DIM
KV_W = N_KV_HEADS * HEAD_DIM
IQ_W = IDX_HEADS * IDX_DIM
SPLITS = (Q_W, KV_W, KV_W, IQ_W, IDX_DIM, IDX_HEADS, SSD_D_INNER, SSD_XBC, SSD_HEADS, D_MODEL, D_MODEL)

LANES = 128
MAIN_W = Q_W + 2 * KV_W + IQ_W + SSD_D_INNER + SSD_XBC + 2 * D_MODEL
COL_Z = Q_W + 2 * KV_W + IQ_W
COL_XBC = COL_Z + SSD_D_INNER
COL_GA = COL_XBC + SSD_XBC
COL_GB = COL_GA + D_MODEL
SM_IW = IDX_DIM
SM_DT = IDX_DIM + IDX_HEADS
SSD_CHUNK = 128
VMEM_LIMIT = 56 * 1024 * 1024
TOKENS_IN_PROJ = 2048
TOKENS_PREP = 512
TOKENS_MERGE = 512
TOKENS_MOE = 2048
INT_MIN = -(2 ** 31)


def _cparams(*sem):
    return pltpu.CompilerParams(dimension_semantics=sem, vmem_limit_bytes=VMEM_LIMIT)


def _split_dot(a, b, parts):
    acc = None
    rem = a
    for _ in range(parts):
        piece = rem.astype(bf16)
        d = jnp.dot(piece, b, preferred_element_type=f32)
        acc = d if acc is None else acc + d
        rem = rem - piece.astype(f32)
    return acc


def _inproj_kernel(x_ref, g_ref, w_ref, ws_ref, o_ref, os_ref, h_ref):
    @pl.when(pl.program_id(1) == 0)
    def _():
        x = x_ref[...]
        ms = jnp.mean(x * x, axis=-1, keepdims=True)
        hb = (x * lax.rsqrt(ms + EPS) * g_ref[...]).astype(bf16)
        h_ref[...] = hb
        os_ref[...] = jnp.dot(hb, ws_ref[...], preferred_element_type=f32)

    o_ref[...] = jnp.dot(h_ref[...], w_ref[...], preferred_element_type=f32).astype(bf16)


def _in_proj(x2d, gain, w_main, w_small):
    n = x2d.shape[0]
    tm = min(TOKENS_IN_PROJ, n)
    tn = 1024
    return pl.pallas_call(
        _inproj_kernel,
        grid=(n // tm, MAIN_W // tn),
        in_specs=[
            pl.BlockSpec((tm, D_MODEL), lambda i, j: (i, 0)),
            pl.BlockSpec((1, D_MODEL), lambda i, j: (0, 0)),
            pl.BlockSpec((D_MODEL, tn), lambda i, j: (0, j)),
            pl.BlockSpec((D_MODEL, LANES), lambda i, j: (0, 0)),
        ],
        out_specs=[
            pl.BlockSpec((tm, tn), lambda i, j: (i, j)),
            pl.BlockSpec((tm, LANES), lambda i, j: (i, 0)),
        ],
        out_shape=[
            jax.ShapeDtypeStruct((n, MAIN_W), bf16),
            jax.ShapeDtypeStruct((n, LANES), f32),
        ],
        scratch_shapes=[pltpu.VMEM((tm, D_MODEL), bf16)],
        compiler_params=_cparams("parallel", "arbitrary"),
        name="in_proj",
    )(x2d, gain, w_main, w_small)


def _rope_tables(t):
    half = ROPE_DIM // 2
    inv = ROPE_THETA ** (-jnp.arange(half, dtype=f32) * 2.0 / ROPE_DIM)
    ang = jnp.arange(t, dtype=f32)[:, None] * inv[None, :]
    cos, sin = jnp.cos(ang), jnp.sin(ang)
    pad = HEAD_DIM - ROPE_DIM
    c = jnp.concatenate([cos, cos, jnp.ones((t, pad), f32)], axis=1)
    s_lo = jnp.concatenate([-sin, jnp.zeros((t, half + pad), f32)], axis=1)
    s_hi = jnp.concatenate([jnp.zeros((t, half), f32), sin, jnp.zeros((t, pad), f32)], axis=1)
    rep = LANES // HEAD_DIM
    return jnp.tile(c, (1, rep)), jnp.tile(s_lo, (1, rep)), jnp.tile(s_hi, (1, rep))


def _rope(x, c, s_lo, s_hi):
    half = ROPE_DIM // 2
    return x * c + pltpu.roll(x, LANES - half, axis=1) * s_lo + pltpu.roll(x, half, axis=1) * s_hi


def _prep_kernel(q_ref, k_ref, v_ref, iq_ref, sm_ref, c_ref, slo_ref, shi_ref, qg_ref, kg_ref, ikg_ref,
                 qo_ref, ko_ref, vo_ref, iqo_ref, iko_ref):
    c, s_lo, s_hi = c_ref[...], slo_ref[...], shi_ref[...]
    r = lax.broadcasted_iota(jnp.int32, (LANES, LANES), 0) // HEAD_DIM
    cc = lax.broadcasted_iota(jnp.int32, (LANES, LANES), 1) // HEAD_DIM
    gsum = jnp.where(r == cc, 1.0, 0.0).astype(bf16)

    def norm_rope(x, gain, scale):
        ms = _split_dot(x * x, gsum, 2) * (1.0 / HEAD_DIM)
        return _rope(x * gain, c, s_lo, s_hi) * (lax.rsqrt(ms + EPS) * scale)

    def store_heads(o_ref, t, y):
        yb = y.astype(bf16)
        o_ref[0, 2 * t] = yb[:, :HEAD_DIM]
        o_ref[0, 2 * t + 1] = yb[:, HEAD_DIM:]

    for t in range(Q_W // LANES):
        sl = slice(t * LANES, (t + 1) * LANES)
        store_heads(qo_ref, t, norm_rope(q_ref[0, :, sl].astype(f32), qg_ref[...], HEAD_DIM ** -0.5))
    for t in range(KV_W // LANES):
        sl = slice(t * LANES, (t + 1) * LANES)
        store_heads(ko_ref, t, norm_rope(k_ref[0, :, sl].astype(f32), kg_ref[...], 1.0))
        store_heads(vo_ref, t, v_ref[0, :, sl])
    for t in range(IQ_W // LANES):
        sl = slice(t * LANES, (t + 1) * LANES)
        store_heads(iqo_ref, t, _rope(iq_ref[0, :, sl].astype(f32), c, s_lo, s_hi) * (IDX_DIM ** -0.5))
    ik = norm_rope(sm_ref[0], ikg_ref[...], 1.0)
    iko_ref[0] = ik[:, :IDX_DIM].astype(bf16)


def _prep(main3, small3, tables, q_gain, k_gain, ik_gain):
    bsz, t_len, _ = main3.shape
    tp = min(TOKENS_PREP, t_len)
    row = lambda w, col: pl.BlockSpec((1, tp, w), lambda b, i, col=col: (b, i, col))
    heads = lambda nh: pl.BlockSpec((1, nh, tp, HEAD_DIM), lambda b, i: (b, 0, i, 0))
    tab = pl.BlockSpec((tp, LANES), lambda b, i: (i, 0))
    par = pl.BlockSpec((1, LANES), lambda b, i: (0, 0))
    hshape = lambda nh: jax.ShapeDtypeStruct((bsz, nh, t_len, HEAD_DIM), bf16)
    return pl.pallas_call(
        _prep_kernel,
        grid=(bsz, t_len // tp),
        in_specs=[row(Q_W, 0), row(KV_W, Q_W // KV_W), row(KV_W, (Q_W + KV_W) // KV_W),
                  row(IQ_W, (Q_W + 2 * KV_W) // IQ_W), row(LANES, 0), tab, tab, tab, par, par, par],
        out_specs=[heads(N_HEADS), heads(N_KV_HEADS), heads(N_KV_HEADS), heads(IDX_HEADS),
                   pl.BlockSpec((1, tp, IDX_DIM), lambda b, i: (b, i, 0))],
        out_shape=[hshape(N_HEADS), hshape(N_KV_HEADS), hshape(N_KV_HEADS), hshape(IDX_HEADS),
                   jax.ShapeDtypeStruct((bsz, t_len, IDX_DIM), bf16)],
        compiler_params=_cparams("parallel", "parallel"),
        name="prep",
    )(main3, main3, main3, main3, small3, *tables, q_gain, k_gain, ik_gain)


def _select_topk(score, topk):
    s, cols = score.shape

    def count(m):
        ones = jnp.where(m, 1.0, 0.0).reshape(s // Q_BLOCK, Q_BLOCK, cols)
        return jnp.sum(jnp.sum(ones, axis=0), axis=0, keepdims=True)

    def as_float(image):
        return lax.bitcast_convert_type(jnp.where(image >= 0, image, image ^ 0x7FFFFFFF), f32)

    t0 = jnp.where(count(score >= 0.0) >= topk, 0, INT_MIN).astype(jnp.int32)

    def vbit(i, t):
        cand = t + jnp.left_shift(jnp.int32(1), 30 - i)
        return jnp.where(count(score >= as_float(cand)) >= topk, cand, t)

    thr = as_float(lax.fori_loop(0, 31, vbit, t0))
    above = score > thr
    ties = score == thr
    need = topk - count(above)
    idx = lax.broadcasted_iota(jnp.int32, (s, cols), 0)
    nbits = max(1, int(math.ceil(math.log2(s))))

    def ibit(i, m):
        cand = m + jnp.left_shift(jnp.int32(1), nbits - 1 - i)
        return jnp.where(count(ties & (idx < cand)) < need, cand, m)

    def tie_break():
        return lax.fori_loop(0, nbits, ibit, jnp.zeros((1, cols), jnp.int32))

    surplus = jnp.max(jnp.where(count(ties) > need, 1.0, 0.0)) > 0.0
    m = lax.cond(surplus, tie_break, lambda: jnp.full((1, cols), s, jnp.int32))
    return above | (ties & (idx <= m))


def _attn_kernel(q_ref, iq_ref, sm_ref, k_ref, v_ref, ik_ref, y_hbm_ref, o_ref, *, s_len, blk, topk):
    del y_hbm_ref
    nt = (((1,), (1,)), ((), ()))
    ik = ik_ref[0]
    iw_t = sm_ref[0].T[SM_IW:SM_IW + IDX_HEADS, :] * (IDX_HEADS ** -0.5)
    score = jnp.zeros((s_len, Q_BLOCK), f32)
    for hp in range(IDX_HEADS // 2):
        iq2 = iq_ref[0, 2 * hp:2 * hp + 2].reshape(2 * Q_BLOCK, IDX_DIM)
        rel = lax.dot_general(ik, iq2, nt, preferred_element_type=f32)
        score = (score + jnp.maximum(rel[:, :Q_BLOCK], 0.0) * iw_t[2 * hp:2 * hp + 1]
                 + jnp.maximum(rel[:, Q_BLOCK:], 0.0) * iw_t[2 * hp + 1:2 * hp + 2])

    k_chunk = lax.broadcasted_iota(jnp.int32, (s_len, Q_BLOCK), 0) // CHUNK
    q_chunk = lax.broadcasted_iota(jnp.int32, (s_len, Q_BLOCK), 1) // CHUNK + blk * (Q_BLOCK // CHUNK)
    adm = k_chunk <= q_chunk
    if s_len <= topk:
        sel = adm
    else:
        sel = _select_topk(jnp.where(adm, score, -jnp.inf), topk)
    bias = jnp.where(sel, 0.0, -jnp.inf).T

    grp = N_HEADS // N_KV_HEADS
    for g in range(N_KV_HEADS):
        qg = q_ref[0, g * grp:(g + 1) * grp].reshape(grp * Q_BLOCK, HEAD_DIM)
        logits = lax.dot_general(qg, k_ref[0, g], nt, preferred_element_type=f32)
        logits = logits.reshape(grp, Q_BLOCK, s_len) + bias[None]
        mx = jnp.max(logits, axis=2, keepdims=True)
        p = jnp.exp(logits - mx)
        den = jnp.sum(p, axis=2, keepdims=True)
        o = jnp.dot(p.reshape(grp * Q_BLOCK, s_len).astype(bf16), v_ref[0, g], preferred_element_type=f32)
        o = o.reshape(grp, Q_BLOCK, HEAD_DIM) / den
        for i in range(grp):
            h = g * grp + i
            o_ref[0, :, h * HEAD_DIM:(h + 1) * HEAD_DIM] = o[i].astype(bf16)


def _attention(qh, kh, vh, iqh, ikn, small3, topk):
    bsz, _, t_len, _ = qh.shape
    assert topk % CHUNK == 0
    y = jnp.zeros((bsz, t_len, Q_W), bf16)
    for blk in range(t_len // Q_BLOCK):
        s_len = (blk + 1) * Q_BLOCK
        qhead = lambda nh, blk=blk: pl.BlockSpec((1, nh, Q_BLOCK, HEAD_DIM), lambda b: (b, 0, blk, 0))
        khead = lambda nh, s_len=s_len: pl.BlockSpec((1, nh, s_len, HEAD_DIM), lambda b: (b, 0, 0, 0))
        in_specs = [qhead(N_HEADS), qhead(IDX_HEADS), pl.BlockSpec((1, Q_BLOCK, LANES), lambda b, blk=blk: (b, blk, 0)),
                    khead(N_KV_HEADS), khead(N_KV_HEADS), pl.BlockSpec((1, s_len, IDX_DIM), lambda b: (b, 0, 0)),
                    pl.BlockSpec(memory_space=pl.ANY)]
        y = pl.pallas_call(
            functools.partial(_attn_kernel, s_len=s_len, blk=blk, topk=topk),
            grid=(bsz,),
            in_specs=in_specs,
            out_specs=pl.BlockSpec((1, Q_BLOCK, Q_W), lambda b, blk=blk: (b, blk, 0)),
            out_shape=jax.ShapeDtypeStruct((bsz, t_len, Q_W), bf16),
            input_output_aliases={len(in_specs) - 1: 0},
            compiler_params=_cparams("parallel"),
            name=f"attn_{blk}",
        )(qh, iqh, small3, kh, vh, ikn, y)
    return y


def _ssd_kernel(xa_ref, xb_ref, xc_ref, z_ref, sm_ref, cw_ref, cb_ref, dtb_ref, alog_ref, dsk_ref, gn_ref,
                o_ref, state_ref, carry_ref, *, clen):
    @pl.when(pl.program_id(1) == 0)
    def _():
        state_ref[...] = jnp.zeros_like(state_ref)
        carry_ref[...] = jnp.zeros_like(carry_ref)

    raw = jnp.concatenate([xa_ref[0], xb_ref[0], xc_ref[0]], axis=1)
    both = jnp.concatenate([carry_ref[...], raw], axis=0)
    carry_ref[...] = raw
    pr = lax.broadcasted_iota(jnp.int32, ((SSD_CONV - 1) * clen, 2 * clen), 0)
    pc = lax.broadcasted_iota(jnp.int32, ((SSD_CONV - 1) * clen, 2 * clen), 1)
    pick = jnp.where(pc == clen + pr % clen - (pr // clen + 1), 1.0, 0.0).astype(bf16)
    shifted = jnp.dot(pick, both, preferred_element_type=f32)
    acc = raw.astype(f32) * cw_ref[SSD_CONV - 1:SSD_CONV, :] + cb_ref[...]
    for s in range(1, SSD_CONV):
        acc = acc + shifted[(s - 1) * clen:s * clen] * cw_ref[SSD_CONV - 1 - s:SSD_CONV - s, :]
    act = acc * jax.nn.sigmoid(acc)
    xs = act[:, :SSD_D_INNER]
    bm = act[:, SSD_D_INNER:SSD_D_INNER + SSD_BC].astype(bf16)
    cm = act[:, SSD_D_INNER + SSD_BC:].astype(bf16)

    dtr = sm_ref[0][:, SM_DT:SM_DT + SSD_HEADS] + dtb_ref[...]
    dt = jnp.maximum(dtr, 0.0) + jnp.log1p(jnp.exp(-jnp.abs(dtr)))
    a = dt * (-jnp.exp(alog_ref[...]))
    ri = lax.broadcasted_iota(jnp.int32, (clen, clen), 0)
    ci = lax.broadcasted_iota(jnp.int32, (clen, clen), 1)
    causal = ci <= ri
    tri = jnp.where(causal, 1.0, 0.0).astype(bf16)
    acs = None
    rem = a
    for _ in range(3):
        piece = rem.astype(bf16)
        d = jnp.dot(tri, piece, preferred_element_type=f32)
        acs = d if acs is None else acs + d
        rem = rem - piece.astype(f32)
    a_last = acs[clen - 1:clen, :]
    acs_t = jnp.concatenate([acs, jnp.zeros((clen, LANES - SSD_HEADS), f32)], axis=1).T

    hh = lax.broadcasted_iota(jnp.int32, (SSD_HEADS, SSD_D_INNER), 0)
    ch = lax.broadcasted_iota(jnp.int32, (SSD_HEADS, SSD_D_INNER), 1) // SSD_HEAD_DIM
    rep = jnp.where(hh == ch, 1.0, 0.0).astype(bf16)
    cd_rows = jnp.broadcast_to(jnp.exp(a_last), (8, SSD_HEADS))
    fac = _split_dot(jnp.concatenate([dt, jnp.exp(a_last - acs), jnp.exp(acs), cd_rows], axis=0), rep, 2)
    dt_e, dte_e, eacs_e, cd_e = fac[:clen], fac[clen:2 * clen], fac[2 * clen:3 * clen], fac[3 * clen:3 * clen + 1]

    xdt = xs * dt_e
    xdt_b = xdt.astype(bf16)
    xw_b = (xdt * dte_e).astype(bf16)
    z = z_ref[0].astype(f32)
    zg = z * jax.nn.sigmoid(z)
    gw = SSD_D_INNER // SSD_GROUPS
    hpg = SSD_HEADS // SSD_GROUPS
    for g in range(SSD_GROUPS):
        gs = slice(g * gw, (g + 1) * gw)
        bg = bm[:, g * SSD_STATE:(g + 1) * SSD_STATE]
        cg = cm[:, g * SSD_STATE:(g + 1) * SSD_STATE]
        cb = lax.dot_general(cg, bg, (((1,), (1,)), ((), ())), preferred_element_type=f32)
        hprev = state_ref[g]
        y_off = jnp.dot(cg, hprev.astype(bf16), preferred_element_type=f32)
        snew = lax.dot_general(bg, xw_b[:, gs], (((0,), (0,)), ((), ())), preferred_element_type=f32)
        state_ref[g] = cd_e[:, gs] * hprev + snew
        parts = []
        for r in range(hpg):
            h = g * hpg + r
            seg = acs[:, h:h + 1] - acs_t[h:h + 1, :]
            dec = jnp.exp(jnp.where(causal, seg, -jnp.inf))
            parts.append(jnp.dot((cb * dec).astype(bf16), xdt_b[:, h * SSD_HEAD_DIM:(h + 1) * SSD_HEAD_DIM],
                                 preferred_element_type=f32))
        y = jnp.concatenate(parts, axis=1) + y_off * eacs_e[:, gs] + dsk_ref[:, gs] * xs[:, gs]
        yz = y * zg[:, gs]
        ms = jnp.mean(yz * yz, axis=-1, keepdims=True)
        o_ref[0, :, gs] = (yz * lax.rsqrt(ms + EPS) * gn_ref[:, gs]).astype(bf16)


def _ssd(main3, small3, conv_w, conv_b, dt_bias, a_log, d_skip_e, ssd_norm):
    bsz, t_len, _ = main3.shape
    clen = min(SSD_CHUNK, t_len)
    xw = SSD_XBC // 3
    xcol = COL_XBC // xw
    blk = lambda w, col: pl.BlockSpec((1, clen, w), lambda b, c, col=col: (b, c, col))
    par = lambda r, w: pl.BlockSpec((r, w), lambda b, c: (0, 0))
    return pl.pallas_call(
        functools.partial(_ssd_kernel, clen=clen),
        grid=(bsz, t_len // clen),
        in_specs=[blk(xw, xcol), blk(xw, xcol + 1), blk(xw, xcol + 2), blk(SSD_D_INNER, COL_Z // SSD_D_INNER),
                  blk(LANES, 0), par(SSD_CONV, SSD_XBC), par(1, SSD_XBC), par(1, SSD_HEADS), par(1, SSD_HEADS),
                  par(1, SSD_D_INNER), par(1, SSD_D_INNER)],
        out_specs=pl.BlockSpec((1, clen, SSD_D_INNER), lambda b, c: (b, c, 0)),
        out_shape=jax.ShapeDtypeStruct((bsz, t_len, SSD_D_INNER), bf16),
        scratch_shapes=[pltpu.VMEM((SSD_GROUPS, SSD_STATE, SSD_D_INNER // SSD_GROUPS), f32),
                        pltpu.VMEM((clen, SSD_XBC), bf16)],
        compiler_params=_cparams("parallel", "arbitrary"),
        name="ssd",
    )(main3, main3, main3, main3, small3, conv_w, conv_b, dt_bias, a_log, d_skip_e, ssd_norm)


def _merge_kernel(ya_ref, ys_ref, ga_ref, gb_ref, x_ref, wa_ref, wb_ref, wo_ref, fg_ref, wr_ref, br_ref,
                  x2_ref, hn_ref, cmb_ref):
    a = jnp.dot(ya_ref[...], wa_ref[...], preferred_element_type=f32)
    s = jnp.dot(ys_ref[...], wb_ref[...], preferred_element_type=f32)
    merged = jax.nn.sigmoid(ga_ref[...].astype(f32)) * a + jax.nn.sigmoid(gb_ref[...].astype(f32)) * s
    x2 = x_ref[...] + jnp.dot(merged.astype(bf16), wo_ref[...], preferred_element_type=f32)
    x2_ref[...] = x2
    ms = jnp.mean(x2 * x2, axis=-1, keepdims=True)
    hn = x2 * lax.rsqrt(ms + EPS) * fg_ref[...]
    hn_hi = hn.astype(bf16)
    hn_ref[...] = hn_hi
    hn_lo = (hn - hn_hi.astype(f32)).astype(bf16)
    hw = jnp.dot(hn_hi, wr_ref[...], preferred_element_type=f32)
    logits = (hw[:, :LANES] + hw[:, LANES:] + jnp.dot(hn_lo, wr_ref[:, :LANES], preferred_element_type=f32)
              + br_ref[...])
    lane = lax.broadcasted_iota(jnp.int32, logits.shape, 1)
    neg = -jnp.inf
    big = jnp.int32(LANES)

    def first_max(mask):
        v = jnp.max(jnp.where(mask, logits, neg), axis=1, keepdims=True)
        i = jnp.min(jnp.where(mask & (logits == v), lane, big), axis=1, keepdims=True)
        return v, i

    is_g = lane < N_EXPERT_GROUPS
    gmax, gidx = first_max(is_g)
    g_w = 1.0 / jnp.sum(jnp.where(is_g, jnp.exp(logits - gmax), 0.0), axis=1, keepdims=True)
    e_lane = lane - N_EXPERT_GROUPS
    in_grp = (e_lane >= gidx * EXPERTS_PER_GROUP) & (e_lane < (gidx + 1) * EXPERTS_PER_GROUP)
    v1, i1 = first_max(in_grp)
    v2, i2 = first_max(in_grp & (lane != i1))
    e2 = jnp.exp(v2 - v1)
    p1 = 1.0 / (1.0 + e2)
    p2 = e2 / (1.0 + e2)
    cmb = jnp.where(lane == i1, g_w * p1, 0.0) + jnp.where(lane == i2, g_w * p2, 0.0)
    cmb_ref[...] = pltpu.roll(cmb, LANES - N_EXPERT_GROUPS, axis=1)


def _merge(y_attn, y_ssd, main, x2d, wa, wb, wo, ffn_gain, w_route, b_route):
    n = x2d.shape[0]
    tm = min(TOKENS_MERGE, n)
    row = lambda w, col=0: pl.BlockSpec((tm, w), lambda i, col=col: (i, col))
    full = lambda r, w: pl.BlockSpec((r, w), lambda i: (0, 0))
    return pl.pallas_call(
        _merge_kernel,
        grid=(n // tm,),
        in_specs=[row(Q_W), row(SSD_D_INNER), row(D_MODEL, COL_GA // D_MODEL), row(D_MODEL, COL_GB // D_MODEL),
                  row(D_MODEL), full(Q_W, D_MODEL), full(SSD_D_INNER, D_MODEL), full(D_MODEL, D_MODEL),
                  full(1, D_MODEL), full(D_MODEL, 2 * LANES), full(1, LANES)],
        out_specs=[row(D_MODEL), row(D_MODEL), row(LANES)],
        out_shape=[
            jax.ShapeDtypeStruct((n, D_MODEL), f32),
            jax.ShapeDtypeStruct((n, D_MODEL), bf16),
            jax.ShapeDtypeStruct((n, LANES), f32),
        ],
        compiler_params=_cparams("parallel"),
        name="merge",
    )(y_attn, y_ssd, main, main, x2d, wa, wb, wo, ffn_gain, w_route, b_route)


def _moe_kernel(hn_ref, cmb_ref, x2_ref, wgu_ref, wd_ref, o_ref):
    e = pl.program_id(1)

    @pl.when(e == 0)
    def _():
        o_ref[...] = x2_ref[...]

    cmb = cmb_ref[...]
    lane = lax.broadcasted_iota(jnp.int32, cmb.shape, 1)
    scale = jnp.sum(jnp.where(lane == e, cmb, 0.0), axis=1, keepdims=True)
    au = jnp.dot(hn_ref[...], wgu_ref[0], preferred_element_type=f32)
    a = au[:, :EXPERT_HIDDEN]
    u = au[:, EXPERT_HIDDEN:]
    act = a * jax.nn.sigmoid(a) * u * scale
    o_ref[...] += jnp.dot(act.astype(bf16), wd_ref[0], preferred_element_type=f32)


def _moe(hn, cmb, x2, wgu, wd):
    n = hn.shape[0]
    tm = min(TOKENS_MOE, n)
    return pl.pallas_call(
        _moe_kernel,
        grid=(n // tm, N_EXPERTS),
        in_specs=[
            pl.BlockSpec((tm, D_MODEL), lambda i, e: (i, 0)),
            pl.BlockSpec((tm, LANES), lambda i, e: (i, 0)),
            pl.BlockSpec((tm, D_MODEL), lambda i, e: (i, 0)),
            pl.BlockSpec((1, D_MODEL, 2 * EXPERT_HIDDEN), lambda i, e: (e, 0, 0)),
            pl.BlockSpec((1, EXPERT_HIDDEN, D_MODEL), lambda i, e: (e, 0, 0)),
        ],
        out_specs=pl.BlockSpec((tm, D_MODEL), lambda i, e: (i, 0)),
        out_shape=jax.ShapeDtypeStruct((n, D_MODEL), f32),
        compiler_params=_cparams("parallel", "arbitrary"),
        name="moe",
    )(hn, cmb, x2, wgu, wd)


def _layer(x, attn_norm, w_in, q_norm, k_norm, idx_k_norm, conv_w, conv_b, dt_bias, a_log, d_skip, ssd_norm,
           w_attn_branch, w_ssd_branch, w_out, ffn_norm, w_route_group, b_route_group, w_route_expert,
           b_route_expert, w_gate, w_up, w_down):
    bsz, t_len, _ = x.shape
    n = bsz * t_len
    offs = np.cumsum(SPLITS)[:-1].tolist()
    (wq, wk, wv, wiq, wik, wiw, wz, wxbc, wdt, wga, wgb) = jnp.split(w_in, offs, axis=-1)
    w_main = jnp.concatenate([wq, wk, wv, wiq, wz, wxbc, wga, wgb], axis=1).astype(bf16)
    pad = LANES - (IDX_DIM + IDX_HEADS + SSD_HEADS)
    w_small = jnp.concatenate([wik, wiw, wdt, jnp.zeros((D_MODEL, pad), f32)], axis=1).astype(bf16)

    x2d = x.reshape(n, D_MODEL)
    main, small = _in_proj(x2d, attn_norm.reshape(1, D_MODEL), w_main, w_small)

    main3 = main.reshape(bsz, t_len, MAIN_W)
    small3 = small.reshape(bsz, t_len, LANES)
    rep = LANES // HEAD_DIM
    qh, kh, vh, iqh, ikn = _prep(main3, small3, _rope_tables(t_len),
                                 jnp.tile(q_norm.reshape(1, HEAD_DIM), (1, rep)),
                                 jnp.tile(k_norm.reshape(1, HEAD_DIM), (1, rep)),
                                 jnp.tile(idx_k_norm.reshape(1, IDX_DIM), (1, rep)))
    y_attn = _attention(qh, kh, vh, iqh, ikn, small3, min(TOPK_MAX, t_len // 4))

    y_ssd = _ssd(main3, small3, conv_w, conv_b.reshape(1, SSD_XBC), dt_bias.reshape(1, SSD_HEADS),
                 a_log.reshape(1, SSD_HEADS), jnp.repeat(d_skip, SSD_HEAD_DIM).reshape(1, SSD_D_INNER),
                 ssd_norm.reshape(1, SSD_D_INNER))

    n_route = N_EXPERT_GROUPS + N_EXPERTS
    w_route = jnp.concatenate([w_route_group, w_route_expert, jnp.zeros((D_MODEL, LANES - n_route), f32)], axis=1)
    w_route_hi = w_route.astype(bf16)
    w_route = jnp.concatenate([w_route_hi, (w_route - w_route_hi.astype(f32)).astype(bf16)], axis=1)
    b_route = jnp.concatenate([b_route_group, b_route_expert, jnp.zeros((LANES - n_route,), f32)]).reshape(1, LANES)
    x2, hn, cmb = _merge(y_attn.reshape(n, Q_W), y_ssd.reshape(n, SSD_D_INNER), main, x2d,
                         w_attn_branch.astype(bf16), w_ssd_branch.astype(bf16), w_out.astype(bf16),
                         ffn_norm.reshape(1, D_MODEL), w_route, b_route)

    wgu = jnp.concatenate([w_gate, w_up], axis=-1).reshape(N_EXPERTS, D_MODEL, 2 * EXPERT_HIDDEN).astype(bf16)
    wd = w_down.reshape(N_EXPERTS, EXPERT_HIDDEN, D_MODEL).astype(bf16)
    out = _moe(hn, cmb, x2, wgu, wd)
    return out.reshape(bsz, t_len, D_MODEL)


def kernel(x, attn_norm, w_in, q_norm, k_norm, idx_k_norm, conv_w, conv_b, dt_bias, a_log, d_skip, ssd_norm,
           w_attn_branch, w_ssd_branch, w_out, ffn_norm, w_route_group, b_route_group, w_route_expert,
           b_route_expert, w_gate, w_up, w_down):
    for l in range(attn_norm.shape[0]):
        x = _layer(x, attn_norm[l], w_in[l], q_norm[l], k_norm[l], idx_k_norm[l], conv_w[l], conv_b[l], dt_bias[l],
                   a_log[l], d_skip[l], ssd_norm[l], w_attn_branch[l], w_ssd_branch[l], w_out[l], ffn_norm[l],
                   w_route_group[l], b_route_group[l], w_route_expert[l], b_route_expert[l], w_gate[l], w_up[l],
                   w_down[l])
    return x
```

```python
import functools
import math

import numpy as np
import jax
import jax.numpy as jnp
from jax import lax
from jax.experimental import pallas as pl
from jax.experimental.pallas import tpu as pltpu

f32 = jnp.float32
bf16 = jnp.bfloat16

D_MODEL = 1024
CHUNK = 64
Q_BLOCK = 128
EPS = 1e-6
N_HEADS = 16
N_KV_HEADS = 4
HEAD_DIM = 64
ROPE_DIM = HEAD_DIM // 4
ROPE_THETA = 500000.0
IDX_HEADS = 8
IDX_DIM = 64
TOPK_MAX = 256
SSD_D_INNER = 2 * D_MODEL
SSD_HEAD_DIM = 64
SSD_HEADS = SSD_D_INNER // SSD_HEAD_DIM
SSD_GROUPS = 4
SSD_STATE = 128
SSD_CONV = 4
SSD_BC = SSD_GROUPS * SSD_STATE
SSD_XBC = SSD_D_INNER + 2 * SSD_BC
N_EXPERT_GROUPS = 4
EXPERTS_PER_GROUP = 8
N_EXPERTS = N_EXPERT_GROUPS * EXPERTS_PER_GROUP
EXPERT_HIDDEN = 256
Q_W = N_HEADS * HEAD_DIM
KV_W = N_KV_HEADS * HEAD_DIM
IQ_W = IDX_HEADS * IDX_DIM
SPLITS = (Q_W, KV_W, KV_W, IQ_W, IDX_DIM, IDX_HEADS, SSD_D_INNER, SSD_XBC, SSD_HEADS, D_MODEL, D_MODEL)

LANES = 128
MAIN_W = Q_W + 2 * KV_W + IQ_W + SSD_D_INNER + SSD_XBC + 2 * D_MODEL
COL_Z = Q_W + 2 * KV_W + IQ_W
COL_XBC = COL_Z + SSD_D_INNER
COL_GA = COL_XBC + SSD_XBC
COL_GB = COL_GA + D_MODEL
SM_IW = IDX_DIM
SM_DT = IDX_DIM + IDX_HEADS
SSD_CHUNK = 128
VMEM_LIMIT = 56 * 1024 * 1024
TOKENS_IN_PROJ = 2048
TOKENS_PREP = 512
TOKENS_MERGE = 512
TOKENS_MOVE = 2048
MOVE_UNROLL = 8
MOE_BLOCK = 1024
ROW_W = D_MODEL + LANES
META_GROUP = EXPERTS_PER_GROUP
META_RANK = EXPERTS_PER_GROUP + 1
INT_MIN = -(2 ** 31)


def _cparams(*sem):
    return pltpu.CompilerParams(dimension_semantics=sem, vmem_limit_bytes=VMEM_LIMIT)


def _split_dot(a, b, parts):
    acc = None
    rem = a
    for _ in range(parts):
        piece = rem.astype(bf16)
        d = jnp.dot(piece, b, preferred_element_type=f32)
        acc = d if acc is None else acc + d
        rem = rem - piece.astype(f32)
    return acc


def _inproj_kernel(x_ref, g_ref, w_ref, ws_ref, o_ref, os_ref, h_ref):
    @pl.when(pl.program_id(1) == 0)
    def _():
        x = x_ref[...]
        ms = jnp.mean(x * x, axis=-1, keepdims=True)
        hb = (x * lax.rsqrt(ms + EPS) * g_ref[...]).astype(bf16)
        h_ref[...] = hb
        os_ref[...] = jnp.dot(hb, ws_ref[...], preferred_element_type=f32)

    o_ref[...] = jnp.dot(h_ref[...], w_ref[...], preferred_element_type=f32).astype(bf16)


def _in_proj(x2d, gain, w_main, w_small):
    n = x2d.shape[0]
    tm = min(TOKENS_IN_PROJ, n)
    tn = 1024
    return pl.pallas_call(
        _inproj_kernel,
        grid=(n // tm, MAIN_W // tn),
        in_specs=[
            pl.BlockSpec((tm, D_MODEL), lambda i, j: (i, 0)),
            pl.BlockSpec((1, D_MODEL), lambda i, j: (0, 0)),
            pl.BlockSpec((D_MODEL, tn), lambda i, j: (0, j)),
            pl.BlockSpec((D_MODEL, LANES), lambda i, j: (0, 0)),
        ],
        out_specs=[
            pl.BlockSpec((tm, tn), lambda i, j: (i, j)),
            pl.BlockSpec((tm, LANES), lambda i, j: (i, 0)),
        ],
        out_shape=[
            jax.ShapeDtypeStruct((n, MAIN_W), bf16),
            jax.ShapeDtypeStruct((n, LANES), f32),
        ],
        scratch_shapes=[pltpu.VMEM((tm, D_MODEL), bf16)],
        compiler_params=_cparams("parallel", "arbitrary"),
        name="in_proj",
    )(x2d, gain, w_main, w_small)


def _rope_tables(t):
    half = ROPE_DIM // 2
    inv = ROPE_THETA ** (-jnp.arange(half, dtype=f32) * 2.0 / ROPE_DIM)
    ang = jnp.arange(t, dtype=f32)[:, None] * inv[None, :]
    cos, sin = jnp.cos(ang), jnp.sin(ang)
    pad = HEAD_DIM - ROPE_DIM
    c = jnp.concatenate([cos, cos, jnp.ones((t, pad), f32)], axis=1)
    s_lo = jnp.concatenate([-sin, jnp.zeros((t, half + pad), f32)], axis=1)
    s_hi = jnp.concatenate([jnp.zeros((t, half), f32), sin, jnp.zeros((t, pad), f32)], axis=1)
    rep = LANES // HEAD_DIM
    return jnp.tile(c, (1, rep)), jnp.tile(s_lo, (1, rep)), jnp.tile(s_hi, (1, rep))


def _rope(x, c, s_lo, s_hi):
    half = ROPE_DIM // 2
    return x * c + pltpu.roll(x, LANES - half, axis=1) * s_lo + pltpu.roll(x, half, axis=1) * s_hi


def _prep_kernel(q_ref, k_ref, v_ref, iq_ref, sm_ref, c_ref, slo_ref, shi_ref, qg_ref, kg_ref, ikg_ref,
                 qo_ref, ko_ref, vo_ref, iqo_ref, iko_ref):
    c, s_lo, s_hi = c_ref[...], slo_ref[...], shi_ref[...]
    r = lax.broadcasted_iota(jnp.int32, (LANES, LANES), 0) // HEAD_DIM
    cc = lax.broadcasted_iota(jnp.int32, (LANES, LANES), 1) // HEAD_DIM
    gsum = jnp.where(r == cc, 1.0, 0.0).astype(bf16)

    def norm_rope(x, gain, scale):
        ms = _split_dot(x * x, gsum, 2) * (1.0 / HEAD_DIM)
        return _rope(x * gain, c, s_lo, s_hi) * (lax.rsqrt(ms + EPS) * scale)

    def store_heads(o_ref, t, y):
        yb = y.astype(bf16)
        o_ref[0, 2 * t] = yb[:, :HEAD_DIM]
        o_ref[0, 2 * t + 1] = yb[:, HEAD_DIM:]

    for t in range(Q_W // LANES):
        sl = slice(t * LANES, (t + 1) * LANES)
        store_heads(qo_ref, t, norm_rope(q_ref[0, :, sl].astype(f32), qg_ref[...], HEAD_DIM ** -0.5))
    for t in range(KV_W // LANES):
        sl = slice(t * LANES, (t + 1) * LANES)
        store_heads(ko_ref, t, norm_rope(k_ref[0, :, sl].astype(f32), kg_ref[...], 1.0))
        store_heads(vo_ref, t, v_ref[0, :, sl])
    for t in range(IQ_W // LANES):
        sl = slice(t * LANES, (t + 1) * LANES)
        store_heads(iqo_ref, t, _rope(iq_ref[0, :, sl].astype(f32), c, s_lo, s_hi) * (IDX_DIM ** -0.5))
    ik = norm_rope(sm_ref[0], ikg_ref[...], 1.0)
    iko_ref[0] = ik[:, :IDX_DIM].astype(bf16)


def _prep(main3, small3, tables, q_gain, k_gain, ik_gain):
    bsz, t_len, _ = main3.shape
    tp = min(TOKENS_PREP, t_len)
    row = lambda w, col: pl.BlockSpec((1, tp, w), lambda b, i, col=col: (b, i, col))
    heads = lambda nh: pl.BlockSpec((1, nh, tp, HEAD_DIM), lambda b, i: (b, 0, i, 0))
    tab = pl.BlockSpec((tp, LANES), lambda b, i: (i, 0))
    par = pl.BlockSpec((1, LANES), lambda b, i: (0, 0))
    hshape = lambda nh: jax.ShapeDtypeStruct((bsz, nh, t_len, HEAD_DIM), bf16)
    return pl.pallas_call(
        _prep_kernel,
        grid=(bsz, t_len // tp),
        in_specs=[row(Q_W, 0), row(KV_W, Q_W // KV_W), row(KV_W, (Q_W + KV_W) // KV_W),
                  row(IQ_W, (Q_W + 2 * KV_W) // IQ_W), row(LANES, 0), tab, tab, tab, par, par, par],
        out_specs=[heads(N_HEADS), heads(N_KV_HEADS), heads(N_KV_HEADS), heads(IDX_HEADS),
                   pl.BlockSpec((1, tp, IDX_DIM), lambda b, i: (b, i, 0))],
        out_shape=[hshape(N_HEADS), hshape(N_KV_HEADS), hshape(N_KV_HEADS), hshape(IDX_HEADS),
                   jax.ShapeDtypeStruct((bsz, t_len, IDX_DIM), bf16)],
        compiler_params=_cparams("parallel", "parallel"),
        name="prep",
    )(main3, main3, main3, main3, small3, *tables, q_gain, k_gain, ik_gain)


def _select_topk(score, topk):
    s, cols = score.shape

    def count(m):
        ones = jnp.where(m, 1.0, 0.0).reshape(s // Q_BLOCK, Q_BLOCK, cols)
        return jnp.sum(jnp.sum(ones, axis=0), axis=0, keepdims=True)

    def as_float(image):
        return lax.bitcast_convert_type(jnp.where(image >= 0, image, image ^ 0x7FFFFFFF), f32)

    t0 = jnp.where(count(score >= 0.0) >= topk, 0, INT_MIN).astype(jnp.int32)

    def vbit(i, t):
        cand = t + jnp.left_shift(jnp.int32(1), 30 - i)
        return jnp.where(count(score >= as_float(cand)) >= topk, cand, t)

    thr = as_float(lax.fori_loop(0, 31, vbit, t0))
    above = score > thr
    ties = score == thr
    need = topk - count(above)
    idx = lax.broadcasted_iota(jnp.int32, (s, cols), 0)
    nbits = max(1, int(math.ceil(math.log2(s))))

    def ibit(i, m):
        cand = m + jnp.left_shift(jnp.int32(1), nbits - 1 - i)
        return jnp.where(count(ties & (idx < cand)) < need, cand, m)

    def tie_break():
        return lax.fori_loop(0, nbits, ibit, jnp.zeros((1, cols), jnp.int32))

    surplus = jnp.max(jnp.where(count(ties) > need, 1.0, 0.0)) > 0.0
    m = lax.cond(surplus, tie_break, lambda: jnp.full((1, cols), s, jnp.int32))
    return above | (ties & (idx <= m))


def _attn_kernel(q_ref, iq_ref, sm_ref, k_ref, v_ref, ik_ref, y_hbm_ref, o_ref, *, s_len, blk, topk):
    del y_hbm_ref
    nt = (((1,), (1,)), ((), ()))
    ik = ik_ref[0]
    iw_t = sm_ref[0].T[SM_IW:SM_IW + IDX_HEADS, :] * (IDX_HEADS ** -0.5)
    score = jnp.zeros((s_len, Q_BLOCK), f32)
    for hp in range(IDX_HEADS // 2):
        iq2 = iq_ref[0, 2 * hp:2 * hp + 2].reshape(2 * Q_BLOCK, IDX_DIM)
        rel = lax.dot_general(ik, iq2, nt, preferred_element_type=f32)
        score = (score + jnp.maximum(rel[:, :Q_BLOCK], 0.0) * iw_t[2 * hp:2 * hp + 1]
                 + jnp.maximum(rel[:, Q_BLOCK:], 0.0) * iw_t[2 * hp + 1:2 * hp + 2])

    k_chunk = lax.broadcasted_iota(jnp.int32, (s_len, Q_BLOCK), 0) // CHUNK
    q_chunk = lax.broadcasted_iota(jnp.int32, (s_len, Q_BLOCK), 1) // CHUNK + blk * (Q_BLOCK // CHUNK)
    adm = k_chunk <= q_chunk
    if s_len <= topk:
        sel = adm
    else:
        sel = _select_topk(jnp.where(adm, score, -jnp.inf), topk)
    bias = jnp.where(sel, 0.0, -jnp.inf).T

    grp = N_HEADS // N_KV_HEADS
    for g in range(N_KV_HEADS):
        qg = q_ref[0, g * grp:(g + 1) * grp].reshape(grp * Q_BLOCK, HEAD_DIM)
        logits = lax.dot_general(qg, k_ref[0, g], nt, preferred_element_type=f32)
        logits = logits.reshape(grp, Q_BLOCK, s_len) + bias[None]
        mx = jnp.max(logits, axis=2, keepdims=True)
        p = jnp.exp(logits - mx)
        den = jnp.sum(p, axis=2, keepdims=True)
        o = jnp.dot(p.reshape(grp * Q_BLOCK, s_len).astype(bf16), v_ref[0, g], preferred_element_type=f32)
        o = o.reshape(grp, Q_BLOCK, HEAD_DIM) / den
        for i in range(grp):
            h = g * grp + i
            o_ref[0, :, h * HEAD_DIM:(h + 1) * HEAD_DIM] = o[i].astype(bf16)


def _attention(qh, kh, vh, iqh, ikn, small3, topk):
    bsz, _, t_len, _ = qh.shape
    assert topk % CHUNK == 0
    y = jnp.zeros((bsz, t_len, Q_W), bf16)
    for blk in range(t_len // Q_BLOCK):
        s_len = (blk + 1) * Q_BLOCK
        qhead = lambda nh, blk=blk: pl.BlockSpec((1, nh, Q_BLOCK, HEAD_DIM), lambda b: (b, 0, blk, 0))
        khead = lambda nh, s_len=s_len: pl.BlockSpec((1, nh, s_len, HEAD_DIM), lambda b: (b, 0, 0, 0))
        in_specs = [qhead(N_HEADS), qhead(IDX_HEADS), pl.BlockSpec((1, Q_BLOCK, LANES), lambda b, blk=blk: (b, blk, 0)),
                    khead(N_KV_HEADS), khead(N_KV_HEADS), pl.BlockSpec((1, s_len, IDX_DIM), lambda b: (b, 0, 0)),
                    pl.BlockSpec(memory_space=pl.ANY)]
        y = pl.pallas_call(
            functools.partial(_attn_kernel, s_len=s_len, blk=blk, topk=topk),
            grid=(bsz,),
            in_specs=in_specs,
            out_specs=pl.BlockSpec((1, Q_BLOCK, Q_W), lambda b, blk=blk: (b, blk, 0)),
            out_shape=jax.ShapeDtypeStruct((bsz, t_len, Q_W), bf16),
            input_output_aliases={len(in_specs) - 1: 0},
            compiler_params=_cparams("parallel"),
            name=f"attn_{blk}",
        )(qh, iqh, small3, kh, vh, ikn, y)
    return y


def _ssd_kernel(xa_ref, xb_ref, xc_ref, z_ref, sm_ref, cw_ref, cb_ref, dtb_ref, alog_ref, dsk_ref, gn_ref,
                o_ref, state_ref, carry_ref, *, clen):
    @pl.when(pl.program_id(1) == 0)
    def _():
        state_ref[...] = jnp.zeros_like(state_ref)
        carry_ref[...] = jnp.zeros_like(carry_ref)

    raw = jnp.concatenate([xa_ref[0], xb_ref[0], xc_ref[0]], axis=1)
    both = jnp.concatenate([carry_ref[...], raw], axis=0)
    carry_ref[...] = raw
    pr = lax.broadcasted_iota(jnp.int32, ((SSD_CONV - 1) * clen, 2 * clen), 0)
    pc = lax.broadcasted_iota(jnp.int32, ((SSD_CONV - 1) * clen, 2 * clen), 1)
    pick = jnp.where(pc == clen + pr % clen - (pr // clen + 1), 1.0, 0.0).astype(bf16)
    shifted = jnp.dot(pick, both, preferred_element_type=f32)
    acc = raw.astype(f32) * cw_ref[SSD_CONV - 1:SSD_CONV, :] + cb_ref[...]
    for s in range(1, SSD_CONV):
        acc = acc + shifted[(s - 1) * clen:s * clen] * cw_ref[SSD_CONV - 1 - s:SSD_CONV - s, :]
    act = acc * jax.nn.sigmoid(acc)
    xs = act[:, :SSD_D_INNER]
    bm = act[:, SSD_D_INNER:SSD_D_INNER + SSD_BC].astype(bf16)
    cm = act[:, SSD_D_INNER + SSD_BC:].astype(bf16)

    dtr = sm_ref[0][:, SM_DT:SM_DT + SSD_HEADS] + dtb_ref[...]
    dt = jnp.maximum(dtr, 0.0) + jnp.log1p(jnp.exp(-jnp.abs(dtr)))
    a = dt * (-jnp.exp(alog_ref[...]))
    ri = lax.broadcasted_iota(jnp.int32, (clen, clen), 0)
    ci = lax.broadcasted_iota(jnp.int32, (clen, clen), 1)
    causal = ci <= ri
    tri = jnp.where(causal, 1.0, 0.0).astype(bf16)
    acs = None
    rem = a
    for _ in range(3):
        piece = rem.astype(bf16)
        d = jnp.dot(tri, piece, preferred_element_type=f32)
        acs = d if acs is None else acs + d
        rem = rem - piece.astype(f32)
    a_last = acs[clen - 1:clen, :]
    acs_t = jnp.concatenate([acs, jnp.zeros((clen, LANES - SSD_HEADS), f32)], axis=1).T

    hh = lax.broadcasted_iota(jnp.int32, (SSD_HEADS, SSD_D_INNER), 0)
    ch = lax.broadcasted_iota(jnp.int32, (SSD_HEADS, SSD_D_INNER), 1) // SSD_HEAD_DIM
    rep = jnp.where(hh == ch, 1.0, 0.0).astype(bf16)
    cd_rows = jnp.broadcast_to(jnp.exp(a_last), (8, SSD_HEADS))
    fac = _split_dot(jnp.concatenate([dt, jnp.exp(a_last - acs), jnp.exp(acs), cd_rows], axis=0), rep, 2)
    dt_e, dte_e, eacs_e, cd_e = fac[:clen], fac[clen:2 * clen], fac[2 * clen:3 * clen], fac[3 * clen:3 * clen + 1]

    xdt = xs * dt_e
    xdt_b = xdt.astype(bf16)
    xw_b = (xdt * dte_e).astype(bf16)
    z = z_ref[0].astype(f32)
    zg = z * jax.nn.sigmoid(z)
    gw = SSD_D_INNER // SSD_GROUPS
    hpg = SSD_HEADS // SSD_GROUPS
    for g in range(SSD_GROUPS):
        gs = slice(g * gw, (g + 1) * gw)
        bg = bm[:, g * SSD_STATE:(g + 1) * SSD_STATE]
        cg = cm[:, g * SSD_STATE:(g + 1) * SSD_STATE]
        cb = lax.dot_general(cg, bg, (((1,), (1,)), ((), ())), preferred_element_type=f32)
        hprev = state_ref[g]
        y_off = jnp.dot(cg, hprev.astype(bf16), preferred_element_type=f32)
        snew = lax.dot_general(bg, xw_b[:, gs], (((0,), (0,)), ((), ())), preferred_element_type=f32)
        state_ref[g] = cd_e[:, gs] * hprev + snew
        parts = []
        for r in range(hpg):
            h = g * hpg + r
            seg = acs[:, h:h + 1] - acs_t[h:h + 1, :]
            dec = jnp.exp(jnp.where(causal, seg, -jnp.inf))
            parts.append(jnp.dot((cb * dec).astype(bf16), xdt_b[:, h * SSD_HEAD_DIM:(h + 1) * SSD_HEAD_DIM],
                                 preferred_element_type=f32))
        y = jnp.concatenate(parts, axis=1) + y_off * eacs_e[:, gs] + dsk_ref[:, gs] * xs[:, gs]
        yz = y * zg[:, gs]
        ms = jnp.mean(yz * yz, axis=-1, keepdims=True)
        o_ref[0, :, gs] = (yz * lax.rsqrt(ms + EPS) * gn_ref[:, gs]).astype(bf16)


def _ssd(main3, small3, conv_w, conv_b, dt_bias, a_log, d_skip_e, ssd_norm):
    bsz, t_len, _ = main3.shape
    clen = min(SSD_CHUNK, t_len)
    xw = SSD_XBC // 3
    xcol = COL_XBC // xw
    blk = lambda w, col: pl.BlockSpec((1, clen, w), lambda b, c, col=col: (b, c, col))
    par = lambda r, w: pl.BlockSpec((r, w), lambda b, c: (0, 0))
    return pl.pallas_call(
        functools.partial(_ssd_kernel, clen=clen),
        grid=(bsz, t_len // clen),
        in_specs=[blk(xw, xcol), blk(xw, xcol + 1), blk(xw, xcol + 2), blk(SSD_D_INNER, COL_Z // SSD_D_INNER),
                  blk(LANES, 0), par(SSD_CONV, SSD_XBC), par(1, SSD_XBC), par(1, SSD_HEADS), par(1, SSD_HEADS),
                  par(1, SSD_D_INNER), par(1, SSD_D_INNER)],
        out_specs=pl.BlockSpec((1, clen, SSD_D_INNER), lambda b, c: (b, c, 0)),
        out_shape=jax.ShapeDtypeStruct((bsz, t_len, SSD_D_INNER), bf16),
        scratch_shapes=[pltpu.VMEM((SSD_GROUPS, SSD_STATE, SSD_D_INNER // SSD_GROUPS), f32),
                        pltpu.VMEM((clen, SSD_XBC), bf16)],
        compiler_params=_cparams("parallel", "arbitrary"),
        name="ssd",
    )(main3, main3, main3, main3, small3, conv_w, conv_b, dt_bias, a_log, d_skip_e, ssd_norm)


def _merge_kernel(ya_ref, ys_ref, ga_ref, gb_ref, x_ref, wa_ref, wb_ref, wo_ref, fg_ref, wr_ref, br_ref,
                  xm_ref, cnt_ref, carry_ref):
    @pl.when(pl.program_id(0) == 0)
    def _():
        carry_ref[...] = jnp.zeros_like(carry_ref)

    a = jnp.dot(ya_ref[...], wa_ref[...], preferred_element_type=f32)
    s = jnp.dot(ys_ref[...], wb_ref[...], preferred_element_type=f32)
    merged = jax.nn.sigmoid(ga_ref[...].astype(f32)) * a + jax.nn.sigmoid(gb_ref[...].astype(f32)) * s
    x2 = x_ref[...] + jnp.dot(merged.astype(bf16), wo_ref[...], preferred_element_type=f32)
    xm_ref[:, :D_MODEL] = x2
    ms = jnp.mean(x2 * x2, axis=-1, keepdims=True)
    hn = x2 * lax.rsqrt(ms + EPS) * fg_ref[...]
    hn_hi = hn.astype(bf16)
    hn_lo = (hn - hn_hi.astype(f32)).astype(bf16)
    hw = jnp.dot(hn_hi, wr_ref[...], preferred_element_type=f32)
    logits = (hw[:, :LANES] + hw[:, LANES:] + jnp.dot(hn_lo, wr_ref[:, :LANES], preferred_element_type=f32)
              + br_ref[...])
    lane = lax.broadcasted_iota(jnp.int32, logits.shape, 1)
    neg = -jnp.inf
    big = jnp.int32(LANES)

    def first_max(mask):
        v = jnp.max(jnp.where(mask, logits, neg), axis=1, keepdims=True)
        i = jnp.min(jnp.where(mask & (logits == v), lane, big), axis=1, keepdims=True)
        return v, i

    is_g = lane < N_EXPERT_GROUPS
    gmax, gidx = first_max(is_g)
    g_w = 1.0 / jnp.sum(jnp.where(is_g, jnp.exp(logits - gmax), 0.0), axis=1, keepdims=True)
    e_lane = lane - N_EXPERT_GROUPS
    in_grp = (e_lane >= gidx * EXPERTS_PER_GROUP) & (e_lane < (gidx + 1) * EXPERTS_PER_GROUP)
    v1, i1 = first_max(in_grp)
    v2, i2 = first_max(in_grp & (lane != i1))
    e2 = jnp.exp(v2 - v1)
    p1 = 1.0 / (1.0 + e2)
    p2 = e2 / (1.0 + e2)
    cmb = jnp.where(lane == i1, g_w * p1, 0.0) + jnp.where(lane == i2, g_w * p2, 0.0)
    cmb8 = jnp.zeros_like(cmb)
    for g in range(N_EXPERT_GROUPS):
        first = N_EXPERT_GROUPS + g * EXPERTS_PER_GROUP
        cmb8 = cmb8 + jnp.where(gidx == g, pltpu.roll(cmb, LANES - first, axis=1), 0.0)
    in_g = jnp.where(lane == gidx, 1.0, 0.0)
    tm = in_g.shape[0]
    earlier = jnp.where(lax.broadcasted_iota(jnp.int32, (tm, tm), 1) < lax.broadcasted_iota(jnp.int32, (tm, tm), 0),
                        1.0, 0.0).astype(bf16)
    before = jnp.dot(earlier, in_g.astype(bf16), preferred_element_type=f32) + carry_ref[0:1, :]
    rank = jnp.sum(in_g * before, axis=1, keepdims=True)
    carry_ref[...] = carry_ref[...] + jnp.sum(in_g, axis=0, keepdims=True)
    cnt_ref[...] = carry_ref[...]
    xm_ref[:, D_MODEL:] = (cmb8 + jnp.where(lane == META_GROUP, gidx.astype(f32), 0.0)
                           + jnp.where(lane == META_RANK, rank, 0.0))


def _merge(y_attn, y_ssd, main, x2d, wa, wb, wo, ffn_gain, w_route, b_route):
    n = x2d.shape[0]
    tm = min(TOKENS_MERGE, n)
    row = lambda w, col=0: pl.BlockSpec((tm, w), lambda i, col=col: (i, col))
    full = lambda r, w: pl.BlockSpec((r, w), lambda i: (0, 0))
    return pl.pallas_call(
        _merge_kernel,
        grid=(n // tm,),
        in_specs=[row(Q_W), row(SSD_D_INNER), row(D_MODEL, COL_GA // D_MODEL), row(D_MODEL, COL_GB // D_MODEL),
                  row(D_MODEL), full(Q_W, D_MODEL), full(SSD_D_INNER, D_MODEL), full(D_MODEL, D_MODEL),
                  full(1, D_MODEL), full(D_MODEL, 2 * LANES), full(1, LANES)],
        out_specs=[row(ROW_W), full(8, LANES)],
        out_shape=[
            jax.ShapeDtypeStruct((n, ROW_W), f32),
            jax.ShapeDtypeStruct((8, LANES), f32),
        ],
        scratch_shapes=[pltpu.VMEM((8, LANES), f32)],
        compiler_params=_cparams("arbitrary"),
        name="merge",
    )(y_attn, y_ssd, main, main, x2d, wa, wb, wo, ffn_gain, w_route, b_route)


def _move_rows_kernel(pos_ref, src_hbm, *rest, rows, scatter):
    dst_hbm, sem = rest[-2:]
    base = pl.program_id(0) * rows

    def copy(t):
        p = pos_ref[base + t]
        s, d = (base + t, p) if scatter else (p, base + t)
        return pltpu.make_async_copy(src_hbm.at[pl.ds(s, 1)], dst_hbm.at[pl.ds(d, 1)], sem)

    def issue(t, carry):
        copy(t).start()
        return carry

    def drain(t, carry):
        copy(t).wait()
        return carry

    lax.fori_loop(0, rows, issue, 0, unroll=MOVE_UNROLL)
    lax.fori_loop(0, rows, drain, 0, unroll=MOVE_UNROLL)


def _move_rows(pos, src, dst_rows, scatter, name):
    n = pos.shape[0]
    rows = min(TOKENS_MOVE, n)
    any_spec = pl.BlockSpec(memory_space=pl.ANY)
    args = [pos, src]
    if scatter:
        args.append(jnp.zeros((dst_rows, src.shape[1]), src.dtype))
    return pl.pallas_call(
        functools.partial(_move_rows_kernel, rows=rows, scatter=scatter),
        grid_spec=pltpu.PrefetchScalarGridSpec(
            num_scalar_prefetch=1, grid=(n // rows,), in_specs=[any_spec] * (len(args) - 1), out_specs=any_spec,
            scratch_shapes=[pltpu.SemaphoreType.DMA(())]),
        out_shape=jax.ShapeDtypeStruct((dst_rows, src.shape[1]), src.dtype),
        input_output_aliases={2: 0} if scatter else {},
        compiler_params=_cparams("arbitrary"),
        name=name,
    )(*args)


def _moe_kernel(grp_ref, nused_ref, row_ref, fg_ref, wgu_ref, wd_ref, o_ref, hn_ref):
    del grp_ref
    e = pl.program_id(1)

    @pl.when(e == 0)
    def _():
        x2 = row_ref[:, :D_MODEL]
        o_ref[...] = x2
        ms = jnp.mean(x2 * x2, axis=-1, keepdims=True)
        hn_ref[...] = (x2 * lax.rsqrt(ms + EPS) * fg_ref[...]).astype(bf16)

    @pl.when(pl.program_id(0) < nused_ref[0])
    def _():
        meta = row_ref[:, D_MODEL:]
        lane = lax.broadcasted_iota(jnp.int32, meta.shape, 1)
        scale = jnp.sum(jnp.where(lane == e, meta, 0.0), axis=1, keepdims=True)
        au = jnp.dot(hn_ref[...], wgu_ref[0], preferred_element_type=f32)
        a = au[:, :EXPERT_HIDDEN]
        u = au[:, EXPERT_HIDDEN:]
        act = a * jax.nn.sigmoid(a) * u * scale
        o_ref[...] += jnp.dot(act.astype(bf16), wd_ref[0], preferred_element_type=f32)


def _moe(rows_sorted, blk, block_group, n_used, ffn_gain, wgu, wd):
    p = rows_sorted.shape[0]
    expert = lambda b, e, grp, nused: (grp[b] * EXPERTS_PER_GROUP + e, 0, 0)
    return pl.pallas_call(
        _moe_kernel,
        grid_spec=pltpu.PrefetchScalarGridSpec(
            num_scalar_prefetch=2,
            grid=(p // blk, EXPERTS_PER_GROUP),
            in_specs=[
                pl.BlockSpec((blk, ROW_W), lambda b, e, grp, nused: (b, 0)),
                pl.BlockSpec((1, D_MODEL), lambda b, e, grp, nused: (0, 0)),
                pl.BlockSpec((1, D_MODEL, 2 * EXPERT_HIDDEN), expert),
                pl.BlockSpec((1, EXPERT_HIDDEN, D_MODEL), expert),
            ],
            out_specs=pl.BlockSpec((blk, D_MODEL), lambda b, e, grp, nused: (b, 0)),
            scratch_shapes=[pltpu.VMEM((blk, D_MODEL), bf16)]),
        out_shape=jax.ShapeDtypeStruct((p, D_MODEL), f32),
        compiler_params=_cparams("arbitrary", "arbitrary"),
        name="moe",
    )(block_group, n_used, rows_sorted, ffn_gain, wgu, wd)


def _routed_moe(xm, counts, ffn_gain, wgu, wd):
    n = xm.shape[0]
    blk = min(MOE_BLOCK, n)
    group = xm[:, D_MODEL + META_GROUP].astype(jnp.int32)
    rank = xm[:, D_MODEL + META_RANK].astype(jnp.int32)
    cnt = counts[0, :N_EXPERT_GROUPS].astype(jnp.int32)
    padded = (cnt + blk - 1) // blk * blk
    ends = jnp.cumsum(padded)
    pos = (ends - padded)[group] + rank
    n_blocks = n // blk + N_EXPERT_GROUPS
    starts = jnp.arange(n_blocks, dtype=jnp.int32) * blk
    block_group = jnp.minimum(jnp.sum(starts[:, None] >= ends[None, :], axis=1), N_EXPERT_GROUPS - 1).astype(jnp.int32)
    n_used = (ends[-1:] // blk).astype(jnp.int32)
    rows_sorted = _move_rows(pos, xm, n_blocks * blk, True, "moe_scatter")
    y_sorted = _moe(rows_sorted, blk, block_group, n_used, ffn_gain, wgu, wd)
    return _move_rows(pos, y_sorted, n, False, "moe_gather")

def _layer(x, attn_norm, w_in, q_norm, k_norm, idx_k_norm, conv_w, conv_b, dt_bias, a_log, d_skip, ssd_norm,
           w_attn_branch, w_ssd_branch, w_out, ffn_norm, w_route_group, b_route_group, w_route_expert,
           b_route_expert, w_gate, w_up, w_down):
    bsz, t_len, _ = x.shape
    n = bsz * t_len
    offs = np.cumsum(SPLITS)[:-1].tolist()
    (wq, wk, wv, wiq, wik, wiw, wz, wxbc, wdt, wga, wgb) = jnp.split(w_in, offs, axis=-1)
    w_main = jnp.concatenate([wq, wk, wv, wiq, wz, wxbc, wga, wgb], axis=1).astype(bf16)
    pad = LANES - (IDX_DIM + IDX_HEADS + SSD_HEADS)
    w_small = jnp.concatenate([wik, wiw, wdt, jnp.zeros((D_MODEL, pad), f32)], axis=1).astype(bf16)

    x2d = x.reshape(n, D_MODEL)
    main, small = _in_proj(x2d, attn_norm.reshape(1, D_MODEL), w_main, w_small)

    main3 = main.reshape(bsz, t_len, MAIN_W)
    small3 = small.reshape(bsz, t_len, LANES)
    rep = LANES // HEAD_DIM
    qh, kh, vh, iqh, ikn = _prep(main3, small3, _rope_tables(t_len),
                                 jnp.tile(q_norm.reshape(1, HEAD_DIM), (1, rep)),
                                 jnp.tile(k_norm.reshape(1, HEAD_DIM), (1, rep)),
                                 jnp.tile(idx_k_norm.reshape(1, IDX_DIM), (1, rep)))
    y_attn = _attention(qh, kh, vh, iqh, ikn, small3, min(TOPK_MAX, t_len // 4))

    y_ssd = _ssd(main3, small3, conv_w, conv_b.reshape(1, SSD_XBC), dt_bias.reshape(1, SSD_HEADS),
                 a_log.reshape(1, SSD_HEADS), jnp.repeat(d_skip, SSD_HEAD_DIM).reshape(1, SSD_D_INNER),
                 ssd_norm.reshape(1, SSD_D_INNER))

    n_route = N_EXPERT_GROUPS + N_EXPERTS
    w_route = jnp.concatenate([w_route_group, w_route_expert, jnp.zeros((D_MODEL, LANES - n_route), f32)], axis=1)
    w_route_hi = w_route.astype(bf16)
    w_route = jnp.concatenate([w_route_hi, (w_route - w_route_hi.astype(f32)).astype(bf16)], axis=1)
    b_route = jnp.concatenate([b_route_group, b_route_expert, jnp.zeros((LANES - n_route,), f32)]).reshape(1, LANES)
    ffn_gain = ffn_norm.reshape(1, D_MODEL)
    xm, counts = _merge(y_attn.reshape(n, Q_W), y_ssd.reshape(n, SSD_D_INNER), main, x2d,
                        w_attn_branch.astype(bf16), w_ssd_branch.astype(bf16), w_out.astype(bf16),
                        ffn_gain, w_route, b_route)

    wgu = jnp.concatenate([w_gate, w_up], axis=-1).reshape(N_EXPERTS, D_MODEL, 2 * EXPERT_HIDDEN).astype(bf16)
    wd = w_down.reshape(N_EXPERTS, EXPERT_HIDDEN, D_MODEL).astype(bf16)
    out = _routed_moe(xm, counts, ffn_gain, wgu, wd)
    return out.reshape(bsz, t_len, D_MODEL)


def kernel(x, attn_norm, w_in, q_norm, k_norm, idx_k_norm, conv_w, conv_b, dt_bias, a_log, d_skip, ssd_norm,
           w_attn_branch, w_ssd_branch, w_out, ffn_norm, w_route_group, b_route_group, w_route_expert,
           b_route_expert, w_gate, w_up, w_down):
    for l in range(attn_norm.shape[0]):
        x = _layer(x, attn_norm[l], w_in[l], q_norm[l], k_norm[l], idx_k_norm[l], conv_w[l], conv_b[l], dt_bias[l],
                   a_log[l], d_skip[l], ssd_norm[l], w_attn_branch[l], w_ssd_branch[l], w_out[l], ffn_norm[l],
                   w_route_group[l], b_route_group[l], w_route_expert[l], b_route_expert[l], w_gate[l], w_up[l],
                   w_down[l])
    return x
```

```python
import functools
import math

import numpy as np
import jax
import jax.numpy as jnp
from jax import lax
from jax.experimental import pallas as pl
from jax.experimental.pallas import tpu as pltpu

f32 = jnp.float32
bf16 = jnp.bfloat16

D_MODEL = 1024
CHUNK = 64
Q_BLOCK = 128
EPS = 1e-6
N_HEADS = 16
N_KV_HEADS = 4
HEAD_DIM = 64
ROPE_DIM = HEAD_DIM // 4
ROPE_THETA = 500000.0
IDX_HEADS = 8
IDX_DIM = 64
TOPK_MAX = 256
SSD_D_INNER = 2 * D_MODEL
SSD_HEAD_DIM = 64
SSD_HEADS = SSD_D_INNER // SSD_HEAD_DIM
SSD_GROUPS = 4
SSD_STATE = 128
SSD_CONV = 4
SSD_BC = SSD_GROUPS * SSD_STATE
SSD_XBC = SSD_D_INNER + 2 * SSD_BC
N_EXPERT_GROUPS = 4
EXPERTS_PER_GROUP = 8
N_EXPERTS = N_EXPERT_GROUPS * EXPERTS_PER_GROUP
EXPERT_HIDDEN = 256
Q_W = N_HEADS * HEAD_DIM
KV_W = N_KV_HEADS * HEAD_DIM
IQ_W = IDX_HEADS * IDX_DIM
SPLITS = (Q_W, KV_W, KV_W, IQ_W, IDX_DIM, IDX_HEADS, SSD_D_INNER, SSD_XBC, SSD_HEADS, D_MODEL, D_MODEL)

LANES = 128
MAIN_W = Q_W + 2 * KV_W + IQ_W + SSD_D_INNER + SSD_XBC + 2 * D_MODEL
COL_Z = Q_W + 2 * KV_W + IQ_W
COL_XBC = COL_Z + SSD_D_INNER
COL_GA = COL_XBC + SSD_XBC
COL_GB = COL_GA + D_MODEL
SM_IW = IDX_DIM
SM_DT = IDX_DIM + IDX_HEADS
SSD_CHUNK = 128
VMEM_LIMIT = 56 * 1024 * 1024
TOKENS_IN_PROJ = 2048
TOKENS_PREP = 512
TOKENS_MERGE = 512
MOVE_UNROLL = 8
MOE_BLOCK = 1024
ROW_W = D_MODEL + LANES
META_GROUP = EXPERTS_PER_GROUP
META_RANK = EXPERTS_PER_GROUP + 1
INT_MIN = -(2 ** 31)


def _cparams(*sem):
    return pltpu.CompilerParams(dimension_semantics=sem, vmem_limit_bytes=VMEM_LIMIT)


def _split_dot(a, b, parts):
    acc = None
    rem = a
    for _ in range(parts):
        piece = rem.astype(bf16)
        d = jnp.dot(piece, b, preferred_element_type=f32)
        acc = d if acc is None else acc + d
        rem = rem - piece.astype(f32)
    return acc


def _inproj_kernel(x_ref, g_ref, w_ref, ws_ref, o_ref, os_ref, h_ref):
    @pl.when(pl.program_id(1) == 0)
    def _():
        x = x_ref[...]
        ms = jnp.mean(x * x, axis=-1, keepdims=True)
        hb = (x * lax.rsqrt(ms + EPS) * g_ref[...]).astype(bf16)
        h_ref[...] = hb
        os_ref[...] = jnp.dot(hb, ws_ref[...], preferred_element_type=f32)

    o_ref[...] = jnp.dot(h_ref[...], w_ref[...], preferred_element_type=f32).astype(bf16)


def _in_proj(x2d, gain, w_main, w_small):
    n = x2d.shape[0]
    tm = min(TOKENS_IN_PROJ, n)
    tn = 1024
    return pl.pallas_call(
        _inproj_kernel,
        grid=(n // tm, MAIN_W // tn),
        in_specs=[
            pl.BlockSpec((tm, D_MODEL), lambda i, j: (i, 0)),
            pl.BlockSpec((1, D_MODEL), lambda i, j: (0, 0)),
            pl.BlockSpec((D_MODEL, tn), lambda i, j: (0, j)),
            pl.BlockSpec((D_MODEL, LANES), lambda i, j: (0, 0)),
        ],
        out_specs=[
            pl.BlockSpec((tm, tn), lambda i, j: (i, j)),
            pl.BlockSpec((tm, LANES), lambda i, j: (i, 0)),
        ],
        out_shape=[
            jax.ShapeDtypeStruct((n, MAIN_W), bf16),
            jax.ShapeDtypeStruct((n, LANES), f32),
        ],
        scratch_shapes=[pltpu.VMEM((tm, D_MODEL), bf16)],
        compiler_params=_cparams("parallel", "arbitrary"),
        name="in_proj",
    )(x2d, gain, w_main, w_small)


def _rope_tables(t):
    half = ROPE_DIM // 2
    inv = ROPE_THETA ** (-jnp.arange(half, dtype=f32) * 2.0 / ROPE_DIM)
    ang = jnp.arange(t, dtype=f32)[:, None] * inv[None, :]
    cos, sin = jnp.cos(ang), jnp.sin(ang)
    pad = HEAD_DIM - ROPE_DIM
    c = jnp.concatenate([cos, cos, jnp.ones((t, pad), f32)], axis=1)
    s_lo = jnp.concatenate([-sin, jnp.zeros((t, half + pad), f32)], axis=1)
    s_hi = jnp.concatenate([jnp.zeros((t, half), f32), sin, jnp.zeros((t, pad), f32)], axis=1)
    rep = LANES // HEAD_DIM
    return jnp.tile(c, (1, rep)), jnp.tile(s_lo, (1, rep)), jnp.tile(s_hi, (1, rep))


def _rope(x, c, s_lo, s_hi):
    half = ROPE_DIM // 2
    return x * c + pltpu.roll(x, LANES - half, axis=1) * s_lo + pltpu.roll(x, half, axis=1) * s_hi


def _prep_kernel(q_ref, k_ref, v_ref, iq_ref, sm_ref, c_ref, slo_ref, shi_ref, qg_ref, kg_ref, ikg_ref,
                 qo_ref, ko_ref, vo_ref, iqo_ref, iko_ref):
    c, s_lo, s_hi = c_ref[...], slo_ref[...], shi_ref[...]
    r = lax.broadcasted_iota(jnp.int32, (LANES, LANES), 0) // HEAD_DIM
    cc = lax.broadcasted_iota(jnp.int32, (LANES, LANES), 1) // HEAD_DIM
    gsum = jnp.where(r == cc, 1.0, 0.0).astype(bf16)

    def norm_rope(x, gain, scale):
        ms = _split_dot(x * x, gsum, 2) * (1.0 / HEAD_DIM)
        return _rope(x * gain, c, s_lo, s_hi) * (lax.rsqrt(ms + EPS) * scale)

    def store_heads(o_ref, t, y):
        yb = y.astype(bf16)
        o_ref[0, 2 * t] = yb[:, :HEAD_DIM]
        o_ref[0, 2 * t + 1] = yb[:, HEAD_DIM:]

    for t in range(Q_W // LANES):
        sl = slice(t * LANES, (t + 1) * LANES)
        store_heads(qo_ref, t, norm_rope(q_ref[0, :, sl].astype(f32), qg_ref[...], HEAD_DIM ** -0.5))
    for t in range(KV_W // LANES):
        sl = slice(t * LANES, (t + 1) * LANES)
        store_heads(ko_ref, t, norm_rope(k_ref[0, :, sl].astype(f32), kg_ref[...], 1.0))
        store_heads(vo_ref, t, v_ref[0, :, sl])
    for t in range(IQ_W // LANES):
        sl = slice(t * LANES, (t + 1) * LANES)
        store_heads(iqo_ref, t, _rope(iq_ref[0, :, sl].astype(f32), c, s_lo, s_hi) * (IDX_DIM ** -0.5))
    ik = norm_rope(sm_ref[0], ikg_ref[...], 1.0)
    iko_ref[0] = ik[:, :IDX_DIM].astype(bf16)


def _prep(main3, small3, tables, q_gain, k_gain, ik_gain):
    bsz, t_len, _ = main3.shape
    tp = min(TOKENS_PREP, t_len)
    row = lambda w, col: pl.BlockSpec((1, tp, w), lambda b, i, col=col: (b, i, col))
    heads = lambda nh: pl.BlockSpec((1, nh, tp, HEAD_DIM), lambda b, i: (b, 0, i, 0))
    tab = pl.BlockSpec((tp, LANES), lambda b, i: (i, 0))
    par = pl.BlockSpec((1, LANES), lambda b, i: (0, 0))
    hshape = lambda nh: jax.ShapeDtypeStruct((bsz, nh, t_len, HEAD_DIM), bf16)
    return pl.pallas_call(
        _prep_kernel,
        grid=(bsz, t_len // tp),
        in_specs=[row(Q_W, 0), row(KV_W, Q_W // KV_W), row(KV_W, (Q_W + KV_W) // KV_W),
                  row(IQ_W, (Q_W + 2 * KV_W) // IQ_W), row(LANES, 0), tab, tab, tab, par, par, par],
        out_specs=[heads(N_HEADS), heads(N_KV_HEADS), heads(N_KV_HEADS), heads(IDX_HEADS),
                   pl.BlockSpec((1, tp, IDX_DIM), lambda b, i: (b, i, 0))],
        out_shape=[hshape(N_HEADS), hshape(N_KV_HEADS), hshape(N_KV_HEADS), hshape(IDX_HEADS),
                   jax.ShapeDtypeStruct((bsz, t_len, IDX_DIM), bf16)],
        compiler_params=_cparams("parallel", "parallel"),
        name="prep",
    )(main3, main3, main3, main3, small3, *tables, q_gain, k_gain, ik_gain)


def _select_topk(score, topk):
    s, cols = score.shape

    def count(m):
        ones = jnp.where(m, 1.0, 0.0).reshape(s // Q_BLOCK, Q_BLOCK, cols)
        return jnp.sum(jnp.sum(ones, axis=0), axis=0, keepdims=True)

    def as_float(image):
        return lax.bitcast_convert_type(jnp.where(image >= 0, image, image ^ 0x7FFFFFFF), f32)

    t0 = jnp.where(count(score >= 0.0) >= topk, 0, INT_MIN).astype(jnp.int32)

    def vbit(i, t):
        cand = t + jnp.left_shift(jnp.int32(1), 30 - i)
        return jnp.where(count(score >= as_float(cand)) >= topk, cand, t)

    thr = as_float(lax.fori_loop(0, 31, vbit, t0))
    above = score > thr
    ties = score == thr
    need = topk - count(above)
    idx = lax.broadcasted_iota(jnp.int32, (s, cols), 0)
    nbits = max(1, int(math.ceil(math.log2(s))))

    def ibit(i, m):
        cand = m + jnp.left_shift(jnp.int32(1), nbits - 1 - i)
        return jnp.where(count(ties & (idx < cand)) < need, cand, m)

    def tie_break():
        return lax.fori_loop(0, nbits, ibit, jnp.zeros((1, cols), jnp.int32))

    surplus = jnp.max(jnp.where(count(ties) > need, 1.0, 0.0)) > 0.0
    m = lax.cond(surplus, tie_break, lambda: jnp.full((1, cols), s, jnp.int32))
    return above | (ties & (idx <= m))


def _attn_kernel(q_ref, iq_ref, sm_ref, k_ref, v_ref, ik_ref, y_hbm_ref, o_ref, *, s_len, blk, topk):
    del y_hbm_ref
    nt = (((1,), (1,)), ((), ()))
    ik = ik_ref[0]
    iw_t = sm_ref[0].T[SM_IW:SM_IW + IDX_HEADS, :] * (IDX_HEADS ** -0.5)
    score = jnp.zeros((s_len, Q_BLOCK), f32)
    for hp in range(IDX_HEADS // 2):
        iq2 = iq_ref[0, 2 * hp:2 * hp + 2].reshape(2 * Q_BLOCK, IDX_DIM)
        rel = lax.dot_general(ik, iq2, nt, preferred_element_type=f32)
        score = (score + jnp.maximum(rel[:, :Q_BLOCK], 0.0) * iw_t[2 * hp:2 * hp + 1]
                 + jnp.maximum(rel[:, Q_BLOCK:], 0.0) * iw_t[2 * hp + 1:2 * hp + 2])

    k_chunk = lax.broadcasted_iota(jnp.int32, (s_len, Q_BLOCK), 0) // CHUNK
    q_chunk = lax.broadcasted_iota(jnp.int32, (s_len, Q_BLOCK), 1) // CHUNK + blk * (Q_BLOCK // CHUNK)
    adm = k_chunk <= q_chunk
    if s_len <= topk:
        sel = adm
    else:
        sel = _select_topk(jnp.where(adm, score, -jnp.inf), topk)
    bias = jnp.where(sel, 0.0, -jnp.inf).T

    grp = N_HEADS // N_KV_HEADS
    for g in range(N_KV_HEADS):
        qg = q_ref[0, g * grp:(g + 1) * grp].reshape(grp * Q_BLOCK, HEAD_DIM)
        logits = lax.dot_general(qg, k_ref[0, g], nt, preferred_element_type=f32)
        logits = logits.reshape(grp, Q_BLOCK, s_len) + bias[None]
        mx = jnp.max(logits, axis=2, keepdims=True)
        p = jnp.exp(logits - mx)
        den = jnp.sum(p, axis=2, keepdims=True)
        o = jnp.dot(p.reshape(grp * Q_BLOCK, s_len).astype(bf16), v_ref[0, g], preferred_element_type=f32)
        o = o.reshape(grp, Q_BLOCK, HEAD_DIM) / den
        for i in range(grp):
            h = g * grp + i
            o_ref[0, :, h * HEAD_DIM:(h + 1) * HEAD_DIM] = o[i].astype(bf16)


def _attention(qh, kh, vh, iqh, ikn, small3, topk):
    bsz, _, t_len, _ = qh.shape
    assert topk % CHUNK == 0
    y = jnp.zeros((bsz, t_len, Q_W), bf16)
    for blk in range(t_len // Q_BLOCK):
        s_len = (blk + 1) * Q_BLOCK
        qhead = lambda nh, blk=blk: pl.BlockSpec((1, nh, Q_BLOCK, HEAD_DIM), lambda b: (b, 0, blk, 0))
        khead = lambda nh, s_len=s_len: pl.BlockSpec((1, nh, s_len, HEAD_DIM), lambda b: (b, 0, 0, 0))
        in_specs = [qhead(N_HEADS), qhead(IDX_HEADS), pl.BlockSpec((1, Q_BLOCK, LANES), lambda b, blk=blk: (b, blk, 0)),
                    khead(N_KV_HEADS), khead(N_KV_HEADS), pl.BlockSpec((1, s_len, IDX_DIM), lambda b: (b, 0, 0)),
                    pl.BlockSpec(memory_space=pl.ANY)]
        y = pl.pallas_call(
            functools.partial(_attn_kernel, s_len=s_len, blk=blk, topk=topk),
            grid=(bsz,),
            in_specs=in_specs,
            out_specs=pl.BlockSpec((1, Q_BLOCK, Q_W), lambda b, blk=blk: (b, blk, 0)),
            out_shape=jax.ShapeDtypeStruct((bsz, t_len, Q_W), bf16),
            input_output_aliases={len(in_specs) - 1: 0},
            compiler_params=_cparams("parallel"),
            name=f"attn_{blk}",
        )(qh, iqh, small3, kh, vh, ikn, y)
    return y


def _ssd_kernel(xa_ref, xb_ref, xc_ref, z_ref, sm_ref, cw_ref, cb_ref, dtb_ref, alog_ref, dsk_ref, gn_ref,
                o_ref, state_ref, carry_ref, *, clen):
    @pl.when(pl.program_id(1) == 0)
    def _():
        state_ref[...] = jnp.zeros_like(state_ref)
        carry_ref[...] = jnp.zeros_like(carry_ref)

    raw = jnp.concatenate([xa_ref[0], xb_ref[0], xc_ref[0]], axis=1)
    both = jnp.concatenate([carry_ref[...], raw], axis=0)
    carry_ref[...] = raw
    pr = lax.broadcasted_iota(jnp.int32, ((SSD_CONV - 1) * clen, 2 * clen), 0)
    pc = lax.broadcasted_iota(jnp.int32, ((SSD_CONV - 1) * clen, 2 * clen), 1)
    pick = jnp.where(pc == clen + pr % clen - (pr // clen + 1), 1.0, 0.0).astype(bf16)
    shifted = jnp.dot(pick, both, preferred_element_type=f32)
    acc = raw.astype(f32) * cw_ref[SSD_CONV - 1:SSD_CONV, :] + cb_ref[...]
    for s in range(1, SSD_CONV):
        acc = acc + shifted[(s - 1) * clen:s * clen] * cw_ref[SSD_CONV - 1 - s:SSD_CONV - s, :]
    act = acc * jax.nn.sigmoid(acc)
    xs = act[:, :SSD_D_INNER]
    bm = act[:, SSD_D_INNER:SSD_D_INNER + SSD_BC].astype(bf16)
    cm = act[:, SSD_D_INNER + SSD_BC:].astype(bf16)

    dtr = sm_ref[0][:, SM_DT:SM_DT + SSD_HEADS] + dtb_ref[...]
    dt = jnp.maximum(dtr, 0.0) + jnp.log1p(jnp.exp(-jnp.abs(dtr)))
    a = dt * (-jnp.exp(alog_ref[...]))
    ri = lax.broadcasted_iota(jnp.int32, (clen, clen), 0)
    ci = lax.broadcasted_iota(jnp.int32, (clen, clen), 1)
    causal = ci <= ri
    tri = jnp.where(causal, 1.0, 0.0).astype(bf16)
    acs = None
    rem = a
    for _ in range(3):
        piece = rem.astype(bf16)
        d = jnp.dot(tri, piece, preferred_element_type=f32)
        acs = d if acs is None else acs + d
        rem = rem - piece.astype(f32)
    a_last = acs[clen - 1:clen, :]
    acs_t = jnp.concatenate([acs, jnp.zeros((clen, LANES - SSD_HEADS), f32)], axis=1).T

    hh = lax.broadcasted_iota(jnp.int32, (SSD_HEADS, SSD_D_INNER), 0)
    ch = lax.broadcasted_iota(jnp.int32, (SSD_HEADS, SSD_D_INNER), 1) // SSD_HEAD_DIM
    rep = jnp.where(hh == ch, 1.0, 0.0).astype(bf16)
    cd_rows = jnp.broadcast_to(jnp.exp(a_last), (8, SSD_HEADS))
    fac = _split_dot(jnp.concatenate([dt, jnp.exp(a_last - acs), jnp.exp(acs), cd_rows], axis=0), rep, 2)
    dt_e, dte_e, eacs_e, cd_e = fac[:clen], fac[clen:2 * clen], fac[2 * clen:3 * clen], fac[3 * clen:3 * clen + 1]

    xdt = xs * dt_e
    xdt_b = xdt.astype(bf16)
    xw_b = (xdt * dte_e).astype(bf16)
    z = z_ref[0].astype(f32)
    zg = z * jax.nn.sigmoid(z)
    gw = SSD_D_INNER // SSD_GROUPS
    hpg = SSD_HEADS // SSD_GROUPS
    for g in range(SSD_GROUPS):
        gs = slice(g * gw, (g + 1) * gw)
        bg = bm[:, g * SSD_STATE:(g + 1) * SSD_STATE]
        cg = cm[:, g * SSD_STATE:(g + 1) * SSD_STATE]
        cb = lax.dot_general(cg, bg, (((1,), (1,)), ((), ())), preferred_element_type=f32)
        hprev = state_ref[g]
        y_off = jnp.dot(cg, hprev.astype(bf16), preferred_element_type=f32)
        snew = lax.dot_general(bg, xw_b[:, gs], (((0,), (0,)), ((), ())), preferred_element_type=f32)
        state_ref[g] = cd_e[:, gs] * hprev + snew
        parts = []
        for r in range(hpg):
            h = g * hpg + r
            seg = acs[:, h:h + 1] - acs_t[h:h + 1, :]
            dec = jnp.exp(jnp.where(causal, seg, -jnp.inf))
            parts.append(jnp.dot((cb * dec).astype(bf16), xdt_b[:, h * SSD_HEAD_DIM:(h + 1) * SSD_HEAD_DIM],
                                 preferred_element_type=f32))
        y = jnp.concatenate(parts, axis=1) + y_off * eacs_e[:, gs] + dsk_ref[:, gs] * xs[:, gs]
        yz = y * zg[:, gs]
        ms = jnp.mean(yz * yz, axis=-1, keepdims=True)
        o_ref[0, :, gs] = (yz * lax.rsqrt(ms + EPS) * gn_ref[:, gs]).astype(bf16)


def _ssd(main3, small3, conv_w, conv_b, dt_bias, a_log, d_skip_e, ssd_norm):
    bsz, t_len, _ = main3.shape
    clen = min(SSD_CHUNK, t_len)
    xw = SSD_XBC // 3
    xcol = COL_XBC // xw
    blk = lambda w, col: pl.BlockSpec((1, clen, w), lambda b, c, col=col: (b, c, col))
    par = lambda r, w: pl.BlockSpec((r, w), lambda b, c: (0, 0))
    return pl.pallas_call(
        functools.partial(_ssd_kernel, clen=clen),
        grid=(bsz, t_len // clen),
        in_specs=[blk(xw, xcol), blk(xw, xcol + 1), blk(xw, xcol + 2), blk(SSD_D_INNER, COL_Z // SSD_D_INNER),
                  blk(LANES, 0), par(SSD_CONV, SSD_XBC), par(1, SSD_XBC), par(1, SSD_HEADS), par(1, SSD_HEADS),
                  par(1, SSD_D_INNER), par(1, SSD_D_INNER)],
        out_specs=pl.BlockSpec((1, clen, SSD_D_INNER), lambda b, c: (b, c, 0)),
        out_shape=jax.ShapeDtypeStruct((bsz, t_len, SSD_D_INNER), bf16),
        scratch_shapes=[pltpu.VMEM((SSD_GROUPS, SSD_STATE, SSD_D_INNER // SSD_GROUPS), f32),
                        pltpu.VMEM((clen, SSD_XBC), bf16)],
        compiler_params=_cparams("parallel", "arbitrary"),
        name="ssd",
    )(main3, main3, main3, main3, small3, conv_w, conv_b, dt_bias, a_log, d_skip_e, ssd_norm)


def _merge_kernel(ya_ref, ys_ref, ga_ref, gb_ref, x_ref, wa_ref, wb_ref, wo_ref, fg_ref, wr_ref, br_ref,
                  xm_ref, cnt_ref, carry_ref):
    @pl.when(pl.program_id(0) == 0)
    def _():
        carry_ref[...] = jnp.zeros_like(carry_ref)

    a = jnp.dot(ya_ref[...], wa_ref[...], preferred_element_type=f32)
    s = jnp.dot(ys_ref[...], wb_ref[...], preferred_element_type=f32)
    merged = jax.nn.sigmoid(ga_ref[...].astype(f32)) * a + jax.nn.sigmoid(gb_ref[...].astype(f32)) * s
    x2 = x_ref[...] + jnp.dot(merged.astype(bf16), wo_ref[...], preferred_element_type=f32)
    xm_ref[:, :D_MODEL] = x2
    ms = jnp.mean(x2 * x2, axis=-1, keepdims=True)
    hn = x2 * lax.rsqrt(ms + EPS) * fg_ref[...]
    hn_hi = hn.astype(bf16)
    hn_lo = (hn - hn_hi.astype(f32)).astype(bf16)
    hw = jnp.dot(hn_hi, wr_ref[...], preferred_element_type=f32)
    logits = (hw[:, :LANES] + hw[:, LANES:] + jnp.dot(hn_lo, wr_ref[:, :LANES], preferred_element_type=f32)
              + br_ref[...])
    lane = lax.broadcasted_iota(jnp.int32, logits.shape, 1)
    neg = -jnp.inf
    big = jnp.int32(LANES)

    def first_max(mask):
        v = jnp.max(jnp.where(mask, logits, neg), axis=1, keepdims=True)
        i = jnp.min(jnp.where(mask & (logits == v), lane, big), axis=1, keepdims=True)
        return v, i

    is_g = lane < N_EXPERT_GROUPS
    gmax, gidx = first_max(is_g)
    g_w = 1.0 / jnp.sum(jnp.where(is_g, jnp.exp(logits - gmax), 0.0), axis=1, keepdims=True)
    e_lane = lane - N_EXPERT_GROUPS
    in_grp = (e_lane >= gidx * EXPERTS_PER_GROUP) & (e_lane < (gidx + 1) * EXPERTS_PER_GROUP)
    v1, i1 = first_max(in_grp)
    v2, i2 = first_max(in_grp & (lane != i1))
    e2 = jnp.exp(v2 - v1)
    p1 = 1.0 / (1.0 + e2)
    p2 = e2 / (1.0 + e2)
    cmb = jnp.where(lane == i1, g_w * p1, 0.0) + jnp.where(lane == i2, g_w * p2, 0.0)
    cmb8 = jnp.zeros_like(cmb)
    for g in range(N_EXPERT_GROUPS):
        first = N_EXPERT_GROUPS + g * EXPERTS_PER_GROUP
        cmb8 = cmb8 + jnp.where(gidx == g, pltpu.roll(cmb, LANES - first, axis=1), 0.0)
    in_g = jnp.where(lane == gidx, 1.0, 0.0)
    tm = in_g.shape[0]
    earlier = jnp.where(lax.broadcasted_iota(jnp.int32, (tm, tm), 1) < lax.broadcasted_iota(jnp.int32, (tm, tm), 0),
                        1.0, 0.0).astype(bf16)
    before = jnp.dot(earlier, in_g.astype(bf16), preferred_element_type=f32) + carry_ref[0:1, :]
    rank = jnp.sum(in_g * before, axis=1, keepdims=True)
    carry_ref[...] = carry_ref[...] + jnp.sum(in_g, axis=0, keepdims=True)
    cnt_ref[...] = carry_ref[...]
    xm_ref[:, D_MODEL:] = (cmb8 + jnp.where(lane == META_GROUP, gidx.astype(f32), 0.0)
                           + jnp.where(lane == META_RANK, rank, 0.0))


def _merge(y_attn, y_ssd, main, x2d, wa, wb, wo, ffn_gain, w_route, b_route):
    n = x2d.shape[0]
    tm = min(TOKENS_MERGE, n)
    row = lambda w, col=0: pl.BlockSpec((tm, w), lambda i, col=col: (i, col))
    full = lambda r, w: pl.BlockSpec((r, w), lambda i: (0, 0))
    return pl.pallas_call(
        _merge_kernel,
        grid=(n // tm,),
        in_specs=[row(Q_W), row(SSD_D_INNER), row(D_MODEL, COL_GA // D_MODEL), row(D_MODEL, COL_GB // D_MODEL),
                  row(D_MODEL), full(Q_W, D_MODEL), full(SSD_D_INNER, D_MODEL), full(D_MODEL, D_MODEL),
                  full(1, D_MODEL), full(D_MODEL, 2 * LANES), full(1, LANES)],
        out_specs=[row(ROW_W), full(8, LANES)],
        out_shape=[
            jax.ShapeDtypeStruct((n, ROW_W), f32),
            jax.ShapeDtypeStruct((8, LANES), f32),
        ],
        scratch_shapes=[pltpu.VMEM((8, LANES), f32)],
        compiler_params=_cparams("arbitrary"),
        name="merge",
    )(y_attn, y_ssd, main, main, x2d, wa, wb, wo, ffn_gain, w_route, b_route)


def _invert_kernel(pos_ref, src_ref, *, n_tokens, n_slots):
    def clear(i, carry):
        src_ref[i] = 0
        return carry

    def put(t, carry):
        src_ref[pos_ref[t]] = t
        return carry

    lax.fori_loop(0, n_slots, clear, 0, unroll=MOVE_UNROLL)
    lax.fori_loop(0, n_tokens, put, 0, unroll=MOVE_UNROLL)


def _invert(pos, n_slots):
    smem = pl.BlockSpec(memory_space=pltpu.SMEM)
    return pl.pallas_call(
        functools.partial(_invert_kernel, n_tokens=pos.shape[0], n_slots=n_slots),
        in_specs=[smem], out_specs=smem,
        out_shape=jax.ShapeDtypeStruct((n_slots,), jnp.int32),
        name="moe_slots",
    )(pos)


def _moe_kernel(grp_ref, nused_ref, nvalid_ref, src_ref, xm_hbm, fg_ref, wgu_ref, wd_ref, out_hbm,
                rows_ref, acc_ref, hn_ref, sem_in, sem_out, *, blk):
    del grp_ref
    b = pl.program_id(0)
    e = pl.program_id(1)
    base = b * blk
    n_valid = nvalid_ref[b]

    def fetch(j):
        return pltpu.make_async_copy(xm_hbm.at[pl.ds(src_ref[base + j], 1)], rows_ref.at[pl.ds(j, 1)], sem_in)

    def emit(j):
        return pltpu.make_async_copy(acc_ref.at[pl.ds(j, 1)], out_hbm.at[pl.ds(src_ref[base + j], 1)], sem_out)

    def for_rows(count, fn):
        def chunk(c, carry):
            for i in range(MOVE_UNROLL):
                fn(c * MOVE_UNROLL + i)
            return carry

        def one(j, carry):
            fn(j)
            return carry

        full = count // MOVE_UNROLL
        lax.fori_loop(0, full, chunk, 0)
        lax.fori_loop(full * MOVE_UNROLL, count, one, 0)

    @pl.when(b < nused_ref[0])
    def _():
        valid = lax.broadcasted_iota(jnp.int32, (blk, 1), 0) < n_valid

        @pl.when(e == 0)
        def _():
            for_rows(blk, lambda j: fetch(j).start())
            for_rows(blk, lambda j: fetch(j).wait())
            x2 = jnp.where(valid, rows_ref[:, :D_MODEL], 0.0)
            acc_ref[...] = x2
            ms = jnp.mean(x2 * x2, axis=-1, keepdims=True)
            hn_ref[...] = (x2 * lax.rsqrt(ms + EPS) * fg_ref[...]).astype(bf16)

        meta = rows_ref[:, D_MODEL:]
        lane = lax.broadcasted_iota(jnp.int32, meta.shape, 1)
        scale = jnp.sum(jnp.where(valid & (lane == e), meta, 0.0), axis=1, keepdims=True)
        au = jnp.dot(hn_ref[...], wgu_ref[0], preferred_element_type=f32)
        a = au[:, :EXPERT_HIDDEN]
        u = au[:, EXPERT_HIDDEN:]
        act = a * jax.nn.sigmoid(a) * u * scale
        acc_ref[...] += jnp.dot(act.astype(bf16), wd_ref[0], preferred_element_type=f32)

        @pl.when(e == EXPERTS_PER_GROUP - 1)
        def _():
            for_rows(n_valid, lambda j: emit(j).start())
            for_rows(n_valid, lambda j: emit(j).wait())


def _moe(xm, blk, block_group, n_used, n_valid, src, ffn_gain, wgu, wd):
    n = xm.shape[0]
    expert = lambda b, e, grp, *_: (grp[b] * EXPERTS_PER_GROUP + e, 0, 0)
    return pl.pallas_call(
        functools.partial(_moe_kernel, blk=blk),
        grid_spec=pltpu.PrefetchScalarGridSpec(
            num_scalar_prefetch=4,
            grid=(src.shape[0] // blk, EXPERTS_PER_GROUP),
            in_specs=[
                pl.BlockSpec(memory_space=pl.ANY),
                pl.BlockSpec((1, D_MODEL), lambda b, e, *_: (0, 0)),
                pl.BlockSpec((1, D_MODEL, 2 * EXPERT_HIDDEN), expert),
                pl.BlockSpec((1, EXPERT_HIDDEN, D_MODEL), expert),
            ],
            out_specs=pl.BlockSpec(memory_space=pl.ANY),
            scratch_shapes=[pltpu.VMEM((blk, ROW_W), f32), pltpu.VMEM((blk, D_MODEL), f32),
                            pltpu.VMEM((blk, D_MODEL), bf16),
                            pltpu.SemaphoreType.DMA(()), pltpu.SemaphoreType.DMA(())]),
        out_shape=jax.ShapeDtypeStruct((n, D_MODEL), f32),
        compiler_params=_cparams("arbitrary", "arbitrary"),
        name="moe",
    )(block_group, n_used, n_valid, src, xm, ffn_gain, wgu, wd)


def _routed_moe(xm, counts, ffn_gain, wgu, wd):
    n = xm.shape[0]
    blk = min(MOE_BLOCK, n)
    group = xm[:, D_MODEL + META_GROUP].astype(jnp.int32)
    rank = xm[:, D_MODEL + META_RANK].astype(jnp.int32)
    cnt = counts[0, :N_EXPERT_GROUPS].astype(jnp.int32)
    padded = (cnt + blk - 1) // blk * blk
    ends = jnp.cumsum(padded)
    offs = ends - padded
    pos = offs[group] + rank
    n_blocks = n // blk + N_EXPERT_GROUPS
    starts = jnp.arange(n_blocks, dtype=jnp.int32) * blk
    block_group = jnp.minimum(jnp.sum(starts[:, None] >= ends[None, :], axis=1), N_EXPERT_GROUPS - 1).astype(jnp.int32)
    n_valid = jnp.clip(cnt[block_group] - (starts - offs[block_group]), 0, blk).astype(jnp.int32)
    n_used = (ends[-1:] // blk).astype(jnp.int32)
    src = _invert(pos, n_blocks * blk)
    return _moe(xm, blk, block_group, n_used, n_valid, src, ffn_gain, wgu, wd)

def _layer(x, attn_norm, w_in, q_norm, k_norm, idx_k_norm, conv_w, conv_b, dt_bias, a_log, d_skip, ssd_norm,
           w_attn_branch, w_ssd_branch, w_out, ffn_norm, w_route_group, b_route_group, w_route_expert,
           b_route_expert, w_gate, w_up, w_down):
    bsz, t_len, _ = x.shape
    n = bsz * t_len
    offs = np.cumsum(SPLITS)[:-1].tolist()
    (wq, wk, wv, wiq, wik, wiw, wz, wxbc, wdt, wga, wgb) = jnp.split(w_in, offs, axis=-1)
    w_main = jnp.concatenate([wq, wk, wv, wiq, wz, wxbc, wga, wgb], axis=1).astype(bf16)
    pad = LANES - (IDX_DIM + IDX_HEADS + SSD_HEADS)
    w_small = jnp.concatenate([wik, wiw, wdt, jnp.zeros((D_MODEL, pad), f32)], axis=1).astype(bf16)

    x2d = x.reshape(n, D_MODEL)
    main, small = _in_proj(x2d, attn_norm.reshape(1, D_MODEL), w_main, w_small)

    main3 = main.reshape(bsz, t_len, MAIN_W)
    small3 = small.reshape(bsz, t_len, LANES)
    rep = LANES // HEAD_DIM
    qh, kh, vh, iqh, ikn = _prep(main3, small3, _rope_tables(t_len),
                                 jnp.tile(q_norm.reshape(1, HEAD_DIM), (1, rep)),
                                 jnp.tile(k_norm.reshape(1, HEAD_DIM), (1, rep)),
                                 jnp.tile(idx_k_norm.reshape(1, IDX_DIM), (1, rep)))
    y_attn = _attention(qh, kh, vh, iqh, ikn, small3, min(TOPK_MAX, t_len // 4))

    y_ssd = _ssd(main3, small3, conv_w, conv_b.reshape(1, SSD_XBC), dt_bias.reshape(1, SSD_HEADS),
                 a_log.reshape(1, SSD_HEADS), jnp.repeat(d_skip, SSD_HEAD_DIM).reshape(1, SSD_D_INNER),
                 ssd_norm.reshape(1, SSD_D_INNER))

    n_route = N_EXPERT_GROUPS + N_EXPERTS
    w_route = jnp.concatenate([w_route_group, w_route_expert, jnp.zeros((D_MODEL, LANES - n_route), f32)], axis=1)
    w_route_hi = w_route.astype(bf16)
    w_route = jnp.concatenate([w_route_hi, (w_route - w_route_hi.astype(f32)).astype(bf16)], axis=1)
    b_route = jnp.concatenate([b_route_group, b_route_expert, jnp.zeros((LANES - n_route,), f32)]).reshape(1, LANES)
    ffn_gain = ffn_norm.reshape(1, D_MODEL)
    xm, counts = _merge(y_attn.reshape(n, Q_W), y_ssd.reshape(n, SSD_D_INNER), main, x2d,
                        w_attn_branch.astype(bf16), w_ssd_branch.astype(bf16), w_out.astype(bf16),
                        ffn_gain, w_route, b_route)

    wgu = jnp.concatenate([w_gate, w_up], axis=-1).reshape(N_EXPERTS, D_MODEL, 2 * EXPERT_HIDDEN).astype(bf16)
    wd = w_down.reshape(N_EXPERTS, EXPERT_HIDDEN, D_MODEL).astype(bf16)
    out = _routed_moe(xm, counts, ffn_gain, wgu, wd)
    return out.reshape(bsz, t_len, D_MODEL)


def kernel(x, attn_norm, w_in, q_norm, k_norm, idx_k_norm, conv_w, conv_b, dt_bias, a_log, d_skip, ssd_norm,
           w_attn_branch, w_ssd_branch, w_out, ffn_norm, w_route_group, b_route_group, w_route_expert,
           b_route_expert, w_gate, w_up, w_down):
    for l in range(attn_norm.shape[0]):
        x = _layer(x, attn_norm[l], w_in[l], q_norm[l], k_norm[l], idx_k_norm[l], conv_w[l], conv_b[l], dt_bias[l],
                   a_log[l], d_skip[l], ssd_norm[l], w_attn_branch[l], w_ssd_branch[l], w_out[l], ffn_norm[l],
                   w_route_group[l], b_route_group[l], w_route_expert[l], b_route_expert[l], w_gate[l], w_up[l],
                   w_down[l])
    return x
```

```python
import functools
import math

import numpy as np
import jax
import jax.numpy as jnp
from jax import lax
from jax.experimental import pallas as pl
from jax.experimental.pallas import tpu as pltpu

f32 = jnp.float32
bf16 = jnp.bfloat16

D_MODEL = 1024
CHUNK = 64
Q_BLOCK = 128
EPS = 1e-6
N_HEADS = 16
N_KV_HEADS = 4
HEAD_DIM = 64
ROPE_DIM = HEAD_DIM // 4
ROPE_THETA = 500000.0
IDX_HEADS = 8
IDX_DIM = 64
TOPK_MAX = 256
SSD_D_INNER = 2 * D_MODEL
SSD_HEAD_DIM = 64
SSD_HEADS = SSD_D_INNER // SSD_HEAD_DIM
SSD_GROUPS = 4
SSD_STATE = 128
SSD_CONV = 4
SSD_BC = SSD_GROUPS * SSD_STATE
SSD_XBC = SSD_D_INNER + 2 * SSD_BC
N_EXPERT_GROUPS = 4
EXPERTS_PER_GROUP = 8
N_EXPERTS = N_EXPERT_GROUPS * EXPERTS_PER_GROUP
EXPERT_HIDDEN = 256
Q_W = N_HEADS * HEAD_DIM
KV_W = N_KV_HEADS * HEAD_DIM
IQ_W = IDX_HEADS * IDX_DIM
SPLITS = (Q_W, KV_W, KV_W, IQ_W, IDX_DIM, IDX_HEADS, SSD_D_INNER, SSD_XBC, SSD_HEADS, D_MODEL, D_MODEL)

LANES = 128
MAIN_W = Q_W + 2 * KV_W + IQ_W + SSD_D_INNER + SSD_XBC + 2 * D_MODEL
COL_Z = Q_W + 2 * KV_W + IQ_W
COL_XBC = COL_Z + SSD_D_INNER
COL_GA = COL_XBC + SSD_XBC
COL_GB = COL_GA + D_MODEL
SM_IW = IDX_DIM
SM_DT = IDX_DIM + IDX_HEADS
SSD_CHUNK = 128
VMEM_LIMIT = 56 * 1024 * 1024
TOKENS_IN_PROJ = 2048
TOKENS_PREP = 512
TOKENS_MERGE = 512
MOVE_UNROLL = 8
MOE_BLOCK = 1024
ROW_W = D_MODEL + LANES
META_GROUP = EXPERTS_PER_GROUP
META_RANK = EXPERTS_PER_GROUP + 1
INT_MIN = -(2 ** 31)


def _cparams(*sem):
    return pltpu.CompilerParams(dimension_semantics=sem, vmem_limit_bytes=VMEM_LIMIT)


def _split_dot(a, b, parts):
    acc = None
    rem = a
    for _ in range(parts):
        piece = rem.astype(bf16)
        d = jnp.dot(piece, b, preferred_element_type=f32)
        acc = d if acc is None else acc + d
        rem = rem - piece.astype(f32)
    return acc


def _inproj_kernel(x_ref, g_ref, w_ref, ws_ref, o_ref, os_ref, h_ref):
    @pl.when(pl.program_id(1) == 0)
    def _():
        x = x_ref[...]
        ms = jnp.mean(x * x, axis=-1, keepdims=True)
        hb = (x * lax.rsqrt(ms + EPS) * g_ref[...]).astype(bf16)
        h_ref[...] = hb
        os_ref[...] = jnp.dot(hb, ws_ref[...], preferred_element_type=f32)

    o_ref[...] = jnp.dot(h_ref[...], w_ref[...], preferred_element_type=f32).astype(bf16)


def _in_proj(x2d, gain, w_main, w_small):
    n = x2d.shape[0]
    tm = min(TOKENS_IN_PROJ, n)
    tn = 1024
    return pl.pallas_call(
        _inproj_kernel,
        grid=(n // tm, MAIN_W // tn),
        in_specs=[
            pl.BlockSpec((tm, D_MODEL), lambda i, j: (i, 0)),
            pl.BlockSpec((1, D_MODEL), lambda i, j: (0, 0)),
            pl.BlockSpec((D_MODEL, tn), lambda i, j: (0, j)),
            pl.BlockSpec((D_MODEL, LANES), lambda i, j: (0, 0)),
        ],
        out_specs=[
            pl.BlockSpec((tm, tn), lambda i, j: (i, j)),
            pl.BlockSpec((tm, LANES), lambda i, j: (i, 0)),
        ],
        out_shape=[
            jax.ShapeDtypeStruct((n, MAIN_W), bf16),
            jax.ShapeDtypeStruct((n, LANES), f32),
        ],
        scratch_shapes=[pltpu.VMEM((tm, D_MODEL), bf16)],
        compiler_params=_cparams("parallel", "arbitrary"),
        name="in_proj",
    )(x2d, gain, w_main, w_small)


def _rope_tables(t):
    half = ROPE_DIM // 2
    inv = ROPE_THETA ** (-jnp.arange(half, dtype=f32) * 2.0 / ROPE_DIM)
    ang = jnp.arange(t, dtype=f32)[:, None] * inv[None, :]
    cos, sin = jnp.cos(ang), jnp.sin(ang)
    pad = HEAD_DIM - ROPE_DIM
    c = jnp.concatenate([cos, cos, jnp.ones((t, pad), f32)], axis=1)
    s_lo = jnp.concatenate([-sin, jnp.zeros((t, half + pad), f32)], axis=1)
    s_hi = jnp.concatenate([jnp.zeros((t, half), f32), sin, jnp.zeros((t, pad), f32)], axis=1)
    rep = LANES // HEAD_DIM
    return jnp.tile(c, (1, rep)), jnp.tile(s_lo, (1, rep)), jnp.tile(s_hi, (1, rep))


def _rope(x, c, s_lo, s_hi):
    half = ROPE_DIM // 2
    return x * c + pltpu.roll(x, LANES - half, axis=1) * s_lo + pltpu.roll(x, half, axis=1) * s_hi


def _prep_kernel(q_ref, k_ref, v_ref, iq_ref, sm_ref, c_ref, slo_ref, shi_ref, qg_ref, kg_ref, ikg_ref,
                 qo_ref, ko_ref, vo_ref, iqo_ref, iko_ref):
    c, s_lo, s_hi = c_ref[...], slo_ref[...], shi_ref[...]
    r = lax.broadcasted_iota(jnp.int32, (LANES, LANES), 0) // HEAD_DIM
    cc = lax.broadcasted_iota(jnp.int32, (LANES, LANES), 1) // HEAD_DIM
    gsum = jnp.where(r == cc, 1.0, 0.0).astype(bf16)

    def norm_rope(x, gain, scale):
        ms = _split_dot(x * x, gsum, 2) * (1.0 / HEAD_DIM)
        return _rope(x * gain, c, s_lo, s_hi) * (lax.rsqrt(ms + EPS) * scale)

    def store_heads(o_ref, t, y):
        yb = y.astype(bf16)
        o_ref[0, 2 * t] = yb[:, :HEAD_DIM]
        o_ref[0, 2 * t + 1] = yb[:, HEAD_DIM:]

    for t in range(Q_W // LANES):
        sl = slice(t * LANES, (t + 1) * LANES)
        store_heads(qo_ref, t, norm_rope(q_ref[0, :, sl].astype(f32), qg_ref[...], HEAD_DIM ** -0.5))
    for t in range(KV_W // LANES):
        sl = slice(t * LANES, (t + 1) * LANES)
        store_heads(ko_ref, t, norm_rope(k_ref[0, :, sl].astype(f32), kg_ref[...], 1.0))
        store_heads(vo_ref, t, v_ref[0, :, sl])
    for t in range(IQ_W // LANES):
        sl = slice(t * LANES, (t + 1) * LANES)
        store_heads(iqo_ref, t, _rope(iq_ref[0, :, sl].astype(f32), c, s_lo, s_hi) * (IDX_DIM ** -0.5))
    ik = norm_rope(sm_ref[0], ikg_ref[...], 1.0)
    iko_ref[0] = ik[:, :IDX_DIM].astype(bf16)


def _prep(main3, small3, tables, q_gain, k_gain, ik_gain):
    bsz, t_len, _ = main3.shape
    tp = min(TOKENS_PREP, t_len)
    row = lambda w, col: pl.BlockSpec((1, tp, w), lambda b, i, col=col: (b, i, col))
    heads = lambda nh: pl.BlockSpec((1, nh, tp, HEAD_DIM), lambda b, i: (b, 0, i, 0))
    tab = pl.BlockSpec((tp, LANES), lambda b, i: (i, 0))
    par = pl.BlockSpec((1, LANES), lambda b, i: (0, 0))
    hshape = lambda nh: jax.ShapeDtypeStruct((bsz, nh, t_len, HEAD_DIM), bf16)
    return pl.pallas_call(
        _prep_kernel,
        grid=(bsz, t_len // tp),
        in_specs=[row(Q_W, 0), row(KV_W, Q_W // KV_W), row(KV_W, (Q_W + KV_W) // KV_W),
                  row(IQ_W, (Q_W + 2 * KV_W) // IQ_W), row(LANES, 0), tab, tab, tab, par, par, par],
        out_specs=[heads(N_HEADS), heads(N_KV_HEADS), heads(N_KV_HEADS), heads(IDX_HEADS),
                   pl.BlockSpec((1, tp, IDX_DIM), lambda b, i: (b, i, 0))],
        out_shape=[hshape(N_HEADS), hshape(N_KV_HEADS), hshape(N_KV_HEADS), hshape(IDX_HEADS),
                   jax.ShapeDtypeStruct((bsz, t_len, IDX_DIM), bf16)],
        compiler_params=_cparams("parallel", "parallel"),
        name="prep",
    )(main3, main3, main3, main3, small3, *tables, q_gain, k_gain, ik_gain)


def _select_topk(score, topk):
    s, cols = score.shape

    def count(m):
        ones = jnp.where(m, 1.0, 0.0).reshape(s // Q_BLOCK, Q_BLOCK, cols)
        return jnp.sum(jnp.sum(ones, axis=0), axis=0, keepdims=True)

    def as_float(image):
        return lax.bitcast_convert_type(jnp.where(image >= 0, image, image ^ 0x7FFFFFFF), f32)

    t0 = jnp.where(count(score >= 0.0) >= topk, 0, INT_MIN).astype(jnp.int32)

    def vbit(i, t):
        cand = t + jnp.left_shift(jnp.int32(1), 30 - i)
        return jnp.where(count(score >= as_float(cand)) >= topk, cand, t)

    thr = as_float(lax.fori_loop(0, 31, vbit, t0))
    above = score > thr
    ties = score == thr
    need = topk - count(above)
    idx = lax.broadcasted_iota(jnp.int32, (s, cols), 0)
    nbits = max(1, int(math.ceil(math.log2(s))))

    def ibit(i, m):
        cand = m + jnp.left_shift(jnp.int32(1), nbits - 1 - i)
        return jnp.where(count(ties & (idx < cand)) < need, cand, m)

    def tie_break():
        return lax.fori_loop(0, nbits, ibit, jnp.zeros((1, cols), jnp.int32))

    surplus = jnp.max(jnp.where(count(ties) > need, 1.0, 0.0)) > 0.0
    m = lax.cond(surplus, tie_break, lambda: jnp.full((1, cols), s, jnp.int32))
    return above | (ties & (idx <= m))


def _attn_kernel(q_ref, iq_ref, sm_ref, k_ref, v_ref, ik_ref, y_hbm_ref, o_ref, *, s_len, blk, topk):
    del y_hbm_ref
    nt = (((1,), (1,)), ((), ()))
    ik = ik_ref[0]
    iw_t = sm_ref[0].T[SM_IW:SM_IW + IDX_HEADS, :] * (IDX_HEADS ** -0.5)
    score = jnp.zeros((s_len, Q_BLOCK), f32)
    for hp in range(IDX_HEADS // 2):
        iq2 = iq_ref[0, 2 * hp:2 * hp + 2].reshape(2 * Q_BLOCK, IDX_DIM)
        rel = lax.dot_general(ik, iq2, nt, preferred_element_type=f32)
        score = (score + jnp.maximum(rel[:, :Q_BLOCK], 0.0) * iw_t[2 * hp:2 * hp + 1]
                 + jnp.maximum(rel[:, Q_BLOCK:], 0.0) * iw_t[2 * hp + 1:2 * hp + 2])

    k_chunk = lax.broadcasted_iota(jnp.int32, (s_len, Q_BLOCK), 0) // CHUNK
    q_chunk = lax.broadcasted_iota(jnp.int32, (s_len, Q_BLOCK), 1) // CHUNK + blk * (Q_BLOCK // CHUNK)
    adm = k_chunk <= q_chunk
    if s_len <= topk:
        sel = adm
    else:
        sel = _select_topk(jnp.where(adm, score, -jnp.inf), topk)
    bias = jnp.where(sel, 0.0, -jnp.inf).T

    grp = N_HEADS // N_KV_HEADS
    for g in range(N_KV_HEADS):
        qg = q_ref[0, g * grp:(g + 1) * grp].reshape(grp * Q_BLOCK, HEAD_DIM)
        logits = lax.dot_general(qg, k_ref[0, g], nt, preferred_element_type=f32)
        logits = logits.reshape(grp, Q_BLOCK, s_len) + bias[None]
        mx = jnp.max(logits, axis=2, keepdims=True)
        p = jnp.exp(logits - mx)
        den = jnp.sum(p, axis=2, keepdims=True)
        o = jnp.dot(p.reshape(grp * Q_BLOCK, s_len).astype(bf16), v_ref[0, g], preferred_element_type=f32)
        o = o.reshape(grp, Q_BLOCK, HEAD_DIM) / den
        for i in range(grp):
            h = g * grp + i
            o_ref[0, :, h * HEAD_DIM:(h + 1) * HEAD_DIM] = o[i].astype(bf16)


def _attention(qh, kh, vh, iqh, ikn, small3, topk):
    bsz, _, t_len, _ = qh.shape
    assert topk % CHUNK == 0
    y = jnp.zeros((bsz, t_len, Q_W), bf16)
    for blk in range(t_len // Q_BLOCK):
        s_len = (blk + 1) * Q_BLOCK
        qhead = lambda nh, blk=blk: pl.BlockSpec((1, nh, Q_BLOCK, HEAD_DIM), lambda b: (b, 0, blk, 0))
        khead = lambda nh, s_len=s_len: pl.BlockSpec((1, nh, s_len, HEAD_DIM), lambda b: (b, 0, 0, 0))
        in_specs = [qhead(N_HEADS), qhead(IDX_HEADS), pl.BlockSpec((1, Q_BLOCK, LANES), lambda b, blk=blk: (b, blk, 0)),
                    khead(N_KV_HEADS), khead(N_KV_HEADS), pl.BlockSpec((1, s_len, IDX_DIM), lambda b: (b, 0, 0)),
                    pl.BlockSpec(memory_space=pl.ANY)]
        y = pl.pallas_call(
            functools.partial(_attn_kernel, s_len=s_len, blk=blk, topk=topk),
            grid=(bsz,),
            in_specs=in_specs,
            out_specs=pl.BlockSpec((1, Q_BLOCK, Q_W), lambda b, blk=blk: (b, blk, 0)),
            out_shape=jax.ShapeDtypeStruct((bsz, t_len, Q_W), bf16),
            input_output_aliases={len(in_specs) - 1: 0},
            compiler_params=_cparams("parallel"),
            name=f"attn_{blk}",
        )(qh, iqh, small3, kh, vh, ikn, y)
    return y


def _ssd_kernel(xa_ref, xb_ref, xc_ref, z_ref, sm_ref, cw_ref, cb_ref, dtb_ref, alog_ref, dsk_ref, gn_ref,
                o_ref, state_ref, carry_ref, *, clen):
    @pl.when(pl.program_id(1) == 0)
    def _():
        state_ref[...] = jnp.zeros_like(state_ref)
        carry_ref[...] = jnp.zeros_like(carry_ref)

    raw = jnp.concatenate([xa_ref[0], xb_ref[0], xc_ref[0]], axis=1)
    both = jnp.concatenate([carry_ref[...], raw], axis=0)
    carry_ref[...] = raw
    pr = lax.broadcasted_iota(jnp.int32, ((SSD_CONV - 1) * clen, 2 * clen), 0)
    pc = lax.broadcasted_iota(jnp.int32, ((SSD_CONV - 1) * clen, 2 * clen), 1)
    pick = jnp.where(pc == clen + pr % clen - (pr // clen + 1), 1.0, 0.0).astype(bf16)
    shifted = jnp.dot(pick, both, preferred_element_type=f32)
    acc = raw.astype(f32) * cw_ref[SSD_CONV - 1:SSD_CONV, :] + cb_ref[...]
    for s in range(1, SSD_CONV):
        acc = acc + shifted[(s - 1) * clen:s * clen] * cw_ref[SSD_CONV - 1 - s:SSD_CONV - s, :]
    act = acc * jax.nn.sigmoid(acc)
    xs = act[:, :SSD_D_INNER]
    bm = act[:, SSD_D_INNER:SSD_D_INNER + SSD_BC].astype(bf16)
    cm = act[:, SSD_D_INNER + SSD_BC:].astype(bf16)

    dtr = sm_ref[0][:, SM_DT:SM_DT + SSD_HEADS] + dtb_ref[...]
    dt = jnp.maximum(dtr, 0.0) + jnp.log1p(jnp.exp(-jnp.abs(dtr)))
    a = dt * (-jnp.exp(alog_ref[...]))
    ri = lax.broadcasted_iota(jnp.int32, (clen, clen), 0)
    ci = lax.broadcasted_iota(jnp.int32, (clen, clen), 1)
    causal = ci <= ri
    tri = jnp.where(causal, 1.0, 0.0).astype(bf16)
    acs = None
    rem = a
    for _ in range(3):
        piece = rem.astype(bf16)
        d = jnp.dot(tri, piece, preferred_element_type=f32)
        acs = d if acs is None else acs + d
        rem = rem - piece.astype(f32)
    a_last = acs[clen - 1:clen, :]
    acs_t = jnp.concatenate([acs, jnp.zeros((clen, LANES - SSD_HEADS), f32)], axis=1).T

    hh = lax.broadcasted_iota(jnp.int32, (SSD_HEADS, SSD_D_INNER), 0)
    ch = lax.broadcasted_iota(jnp.int32, (SSD_HEADS, SSD_D_INNER), 1) // SSD_HEAD_DIM
    rep = jnp.where(hh == ch, 1.0, 0.0).astype(bf16)
    cd_rows = jnp.broadcast_to(jnp.exp(a_last), (8, SSD_HEADS))
    fac = _split_dot(jnp.concatenate([dt, jnp.exp(a_last - acs), jnp.exp(acs), cd_rows], axis=0), rep, 2)
    dt_e, dte_e, eacs_e, cd_e = fac[:clen], fac[clen:2 * clen], fac[2 * clen:3 * clen], fac[3 * clen:3 * clen + 1]

    xdt = xs * dt_e
    xdt_b = xdt.astype(bf16)
    xw_b = (xdt * dte_e).astype(bf16)
    z = z_ref[0].astype(f32)
    zg = z * jax.nn.sigmoid(z)
    gw = SSD_D_INNER // SSD_GROUPS
    hpg = SSD_HEADS // SSD_GROUPS
    for g in range(SSD_GROUPS):
        gs = slice(g * gw, (g + 1) * gw)
        bg = bm[:, g * SSD_STATE:(g + 1) * SSD_STATE]
        cg = cm[:, g * SSD_STATE:(g + 1) * SSD_STATE]
        cb = lax.dot_general(cg, bg, (((1,), (1,)), ((), ())), preferred_element_type=f32)
        hprev = state_ref[g]
        y_off = jnp.dot(cg, hprev.astype(bf16), preferred_element_type=f32)
        snew = lax.dot_general(bg, xw_b[:, gs], (((0,), (0,)), ((), ())), preferred_element_type=f32)
        state_ref[g] = cd_e[:, gs] * hprev + snew
        parts = []
        for r in range(hpg):
            h = g * hpg + r
            seg = acs[:, h:h + 1] - acs_t[h:h + 1, :]
            dec = jnp.exp(jnp.where(causal, seg, -jnp.inf))
            parts.append(jnp.dot((cb * dec).astype(bf16), xdt_b[:, h * SSD_HEAD_DIM:(h + 1) * SSD_HEAD_DIM],
                                 preferred_element_type=f32))
        y = jnp.concatenate(parts, axis=1) + y_off * eacs_e[:, gs] + dsk_ref[:, gs] * xs[:, gs]
        yz = y * zg[:, gs]
        ms = jnp.mean(yz * yz, axis=-1, keepdims=True)
        o_ref[0, :, gs] = (yz * lax.rsqrt(ms + EPS) * gn_ref[:, gs]).astype(bf16)


def _ssd(main3, small3, conv_w, conv_b, dt_bias, a_log, d_skip_e, ssd_norm):
    bsz, t_len, _ = main3.shape
    clen = min(SSD_CHUNK, t_len)
    xw = SSD_XBC // 3
    xcol = COL_XBC // xw
    blk = lambda w, col: pl.BlockSpec((1, clen, w), lambda b, c, col=col: (b, c, col))
    par = lambda r, w: pl.BlockSpec((r, w), lambda b, c: (0, 0))
    return pl.pallas_call(
        functools.partial(_ssd_kernel, clen=clen),
        grid=(bsz, t_len // clen),
        in_specs=[blk(xw, xcol), blk(xw, xcol + 1), blk(xw, xcol + 2), blk(SSD_D_INNER, COL_Z // SSD_D_INNER),
                  blk(LANES, 0), par(SSD_CONV, SSD_XBC), par(1, SSD_XBC), par(1, SSD_HEADS), par(1, SSD_HEADS),
                  par(1, SSD_D_INNER), par(1, SSD_D_INNER)],
        out_specs=pl.BlockSpec((1, clen, SSD_D_INNER), lambda b, c: (b, c, 0)),
        out_shape=jax.ShapeDtypeStruct((bsz, t_len, SSD_D_INNER), bf16),
        scratch_shapes=[pltpu.VMEM((SSD_GROUPS, SSD_STATE, SSD_D_INNER // SSD_GROUPS), f32),
                        pltpu.VMEM((clen, SSD_XBC), bf16)],
        compiler_params=_cparams("parallel", "arbitrary"),
        name="ssd",
    )(main3, main3, main3, main3, small3, conv_w, conv_b, dt_bias, a_log, d_skip_e, ssd_norm)


def _merge_kernel(ya_ref, ys_ref, ga_ref, gb_ref, x_ref, wa_ref, wb_ref, wo_ref, fg_ref, wr_ref, br_ref,
                  xm_ref, cnt_ref, carry_ref):
    @pl.when(pl.program_id(0) == 0)
    def _():
        carry_ref[...] = jnp.zeros_like(carry_ref)

    a = jnp.dot(ya_ref[...], wa_ref[...], preferred_element_type=f32)
    s = jnp.dot(ys_ref[...], wb_ref[...], preferred_element_type=f32)
    merged = jax.nn.sigmoid(ga_ref[...].astype(f32)) * a + jax.nn.sigmoid(gb_ref[...].astype(f32)) * s
    x2 = x_ref[...] + jnp.dot(merged.astype(bf16), wo_ref[...], preferred_element_type=f32)
    xm_ref[:, :D_MODEL] = x2
    ms = jnp.mean(x2 * x2, axis=-1, keepdims=True)
    hn = x2 * lax.rsqrt(ms + EPS) * fg_ref[...]
    hn_hi = hn.astype(bf16)
    hn_lo = (hn - hn_hi.astype(f32)).astype(bf16)
    hw = jnp.dot(hn_hi, wr_ref[...], preferred_element_type=f32)
    logits = (hw[:, :LANES] + hw[:, LANES:] + jnp.dot(hn_lo, wr_ref[:, :LANES], preferred_element_type=f32)
              + br_ref[...])
    lane = lax.broadcasted_iota(jnp.int32, logits.shape, 1)
    neg = -jnp.inf
    big = jnp.int32(LANES)

    def first_max(mask):
        v = jnp.max(jnp.where(mask, logits, neg), axis=1, keepdims=True)
        i = jnp.min(jnp.where(mask & (logits == v), lane, big), axis=1, keepdims=True)
        return v, i

    is_g = lane < N_EXPERT_GROUPS
    gmax, gidx = first_max(is_g)
    g_w = 1.0 / jnp.sum(jnp.where(is_g, jnp.exp(logits - gmax), 0.0), axis=1, keepdims=True)
    e_lane = lane - N_EXPERT_GROUPS
    in_grp = (e_lane >= gidx * EXPERTS_PER_GROUP) & (e_lane < (gidx + 1) * EXPERTS_PER_GROUP)
    v1, i1 = first_max(in_grp)
    v2, i2 = first_max(in_grp & (lane != i1))
    e2 = jnp.exp(v2 - v1)
    p1 = 1.0 / (1.0 + e2)
    p2 = e2 / (1.0 + e2)
    cmb = jnp.where(lane == i1, g_w * p1, 0.0) + jnp.where(lane == i2, g_w * p2, 0.0)
    cmb8 = jnp.zeros_like(cmb)
    for g in range(N_EXPERT_GROUPS):
        first = N_EXPERT_GROUPS + g * EXPERTS_PER_GROUP
        cmb8 = cmb8 + jnp.where(gidx == g, pltpu.roll(cmb, LANES - first, axis=1), 0.0)
    in_g = jnp.where(lane == gidx, 1.0, 0.0)
    tm = in_g.shape[0]
    earlier = jnp.where(lax.broadcasted_iota(jnp.int32, (tm, tm), 1) < lax.broadcasted_iota(jnp.int32, (tm, tm), 0),
                        1.0, 0.0).astype(bf16)
    before = jnp.dot(earlier, in_g.astype(bf16), preferred_element_type=f32) + carry_ref[0:1, :]
    rank = jnp.sum(in_g * before, axis=1, keepdims=True)
    carry_ref[...] = carry_ref[...] + jnp.sum(in_g, axis=0, keepdims=True)
    cnt_ref[...] = carry_ref[...]
    xm_ref[:, D_MODEL:] = (cmb8 + jnp.where(lane == META_GROUP, gidx.astype(f32), 0.0)
                           + jnp.where(lane == META_RANK, rank, 0.0))


def _merge(y_attn, y_ssd, main, x2d, wa, wb, wo, ffn_gain, w_route, b_route):
    n = x2d.shape[0]
    tm = min(TOKENS_MERGE, n)
    row = lambda w, col=0: pl.BlockSpec((tm, w), lambda i, col=col: (i, col))
    full = lambda r, w: pl.BlockSpec((r, w), lambda i: (0, 0))
    return pl.pallas_call(
        _merge_kernel,
        grid=(n // tm,),
        in_specs=[row(Q_W), row(SSD_D_INNER), row(D_MODEL, COL_GA // D_MODEL), row(D_MODEL, COL_GB // D_MODEL),
                  row(D_MODEL), full(Q_W, D_MODEL), full(SSD_D_INNER, D_MODEL), full(D_MODEL, D_MODEL),
                  full(1, D_MODEL), full(D_MODEL, 2 * LANES), full(1, LANES)],
        out_specs=[row(ROW_W), full(8, LANES)],
        out_shape=[
            jax.ShapeDtypeStruct((n, ROW_W), f32),
            jax.ShapeDtypeStruct((8, LANES), f32),
        ],
        scratch_shapes=[pltpu.VMEM((8, LANES), f32)],
        compiler_params=_cparams("arbitrary"),
        name="merge",
    )(y_attn, y_ssd, main, main, x2d, wa, wb, wo, ffn_gain, w_route, b_route)


def _invert_kernel(pos_ref, src_ref, *, n_tokens, n_slots):
    def clear(i, carry):
        src_ref[i] = 0
        return carry

    def put(t, carry):
        src_ref[pos_ref[t]] = t
        return carry

    lax.fori_loop(0, n_slots, clear, 0, unroll=MOVE_UNROLL)
    lax.fori_loop(0, n_tokens, put, 0, unroll=MOVE_UNROLL)


def _invert(pos, n_slots):
    smem = pl.BlockSpec(memory_space=pltpu.SMEM)
    return pl.pallas_call(
        functools.partial(_invert_kernel, n_tokens=pos.shape[0], n_slots=n_slots),
        in_specs=[smem], out_specs=smem,
        out_shape=jax.ShapeDtypeStruct((n_slots,), jnp.int32),
        name="moe_slots",
    )(pos)


def _moe_kernel(grp_ref, nused_ref, nvalid_ref, src_ref, xm_hbm, fg_ref, wgu_ref, wd_ref, out_hbm,
                rows_ref, acc_ref, hn_ref, sem_in, sem_out, *, blk):
    del grp_ref
    b = pl.program_id(0)
    e = pl.program_id(1)
    n_used = nused_ref[0]
    n_valid = nvalid_ref[b]
    buf = b % 2

    def fetch(blk_id, j):
        which = blk_id % 2
        return pltpu.make_async_copy(xm_hbm.at[pl.ds(src_ref[blk_id * blk + j], 1)],
                                     rows_ref.at[which, pl.ds(j, 1)], sem_in.at[which])

    def emit(blk_id, j):
        return pltpu.make_async_copy(acc_ref.at[pl.ds(j, 1)],
                                     out_hbm.at[pl.ds(src_ref[blk_id * blk + j], 1)], sem_out)

    def for_rows(count, fn):
        def chunk(c, carry):
            for i in range(MOVE_UNROLL):
                fn(c * MOVE_UNROLL + i)
            return carry

        def one(j, carry):
            fn(j)
            return carry

        full = count // MOVE_UNROLL
        lax.fori_loop(0, full, chunk, 0)
        lax.fori_loop(full * MOVE_UNROLL, count, one, 0)

    @pl.when(b < n_used)
    def _():
        valid = lax.broadcasted_iota(jnp.int32, (blk, 1), 0) < n_valid

        @pl.when(e == 0)
        def _():
            @pl.when(b == 0)
            def _():
                for_rows(blk, lambda j: fetch(b, j).start())
            for_rows(blk, lambda j: fetch(b, j).wait())

            @pl.when(b > 0)
            def _():
                for_rows(nvalid_ref[b - 1], lambda j: emit(b - 1, j).wait())
            x2 = jnp.where(valid, rows_ref[buf, :, :D_MODEL], 0.0)
            acc_ref[...] = x2
            ms = jnp.mean(x2 * x2, axis=-1, keepdims=True)
            hn_ref[...] = (x2 * lax.rsqrt(ms + EPS) * fg_ref[...]).astype(bf16)

        @pl.when((e == 1) & (b + 1 < n_used))
        def _():
            for_rows(blk, lambda j: fetch(b + 1, j).start())

        meta = rows_ref[buf, :, D_MODEL:]
        lane = lax.broadcasted_iota(jnp.int32, meta.shape, 1)
        scale = jnp.sum(jnp.where(valid & (lane == e), meta, 0.0), axis=1, keepdims=True)
        au = jnp.dot(hn_ref[...], wgu_ref[0], preferred_element_type=f32)
        a = au[:, :EXPERT_HIDDEN]
        u = au[:, EXPERT_HIDDEN:]
        act = a * jax.nn.sigmoid(a) * u * scale
        acc_ref[...] += jnp.dot(act.astype(bf16), wd_ref[0], preferred_element_type=f32)

        @pl.when(e == EXPERTS_PER_GROUP - 1)
        def _():
            for_rows(n_valid, lambda j: emit(b, j).start())

            @pl.when(b == n_used - 1)
            def _():
                for_rows(n_valid, lambda j: emit(b, j).wait())


def _moe(xm, blk, block_group, n_used, n_valid, src, ffn_gain, wgu, wd):
    n = xm.shape[0]
    expert = lambda b, e, grp, *_: (grp[b] * EXPERTS_PER_GROUP + e, 0, 0)
    return pl.pallas_call(
        functools.partial(_moe_kernel, blk=blk),
        grid_spec=pltpu.PrefetchScalarGridSpec(
            num_scalar_prefetch=4,
            grid=(src.shape[0] // blk, EXPERTS_PER_GROUP),
            in_specs=[
                pl.BlockSpec(memory_space=pl.ANY),
                pl.BlockSpec((1, D_MODEL), lambda b, e, *_: (0, 0)),
                pl.BlockSpec((1, D_MODEL, 2 * EXPERT_HIDDEN), expert),
                pl.BlockSpec((1, EXPERT_HIDDEN, D_MODEL), expert),
            ],
            out_specs=pl.BlockSpec(memory_space=pl.ANY),
            scratch_shapes=[pltpu.VMEM((2, blk, ROW_W), f32), pltpu.VMEM((blk, D_MODEL), f32),
                            pltpu.VMEM((blk, D_MODEL), bf16),
                            pltpu.SemaphoreType.DMA((2,)), pltpu.SemaphoreType.DMA(())]),
        out_shape=jax.ShapeDtypeStruct((n, D_MODEL), f32),
        compiler_params=_cparams("arbitrary", "arbitrary"),
        name="moe",
    )(block_group, n_used, n_valid, src, xm, ffn_gain, wgu, wd)


def _routed_moe(xm, counts, ffn_gain, wgu, wd):
    n = xm.shape[0]
    blk = min(MOE_BLOCK, n)
    group = xm[:, D_MODEL + META_GROUP].astype(jnp.int32)
    rank = xm[:, D_MODEL + META_RANK].astype(jnp.int32)
    cnt = counts[0, :N_EXPERT_GROUPS].astype(jnp.int32)
    padded = (cnt + blk - 1) // blk * blk
    ends = jnp.cumsum(padded)
    offs = ends - padded
    pos = offs[group] + rank
    n_blocks = n // blk + N_EXPERT_GROUPS
    starts = jnp.arange(n_blocks, dtype=jnp.int32) * blk
    block_group = jnp.minimum(jnp.sum(starts[:, None] >= ends[None, :], axis=1), N_EXPERT_GROUPS - 1).astype(jnp.int32)
    n_valid = jnp.clip(cnt[block_group] - (starts - offs[block_group]), 0, blk).astype(jnp.int32)
    n_used = (ends[-1:] // blk).astype(jnp.int32)
    src = _invert(pos, n_blocks * blk)
    return _moe(xm, blk, block_group, n_used, n_valid, src, ffn_gain, wgu, wd)

def _layer(x, attn_norm, w_in, q_norm, k_norm, idx_k_norm, conv_w, conv_b, dt_bias, a_log, d_skip, ssd_norm,
           w_attn_branch, w_ssd_branch, w_out, ffn_norm, w_route_group, b_route_group, w_route_expert,
           b_route_expert, w_gate, w_up, w_down):
    bsz, t_len, _ = x.shape
    n = bsz * t_len
    offs = np.cumsum(SPLITS)[:-1].tolist()
    (wq, wk, wv, wiq, wik, wiw, wz, wxbc, wdt, wga, wgb) = jnp.split(w_in, offs, axis=-1)
    w_main = jnp.concatenate([wq, wk, wv, wiq, wz, wxbc, wga, wgb], axis=1).astype(bf16)
    pad = LANES - (IDX_DIM + IDX_HEADS + SSD_HEADS)
    w_small = jnp.concatenate([wik, wiw, wdt, jnp.zeros((D_MODEL, pad), f32)], axis=1).astype(bf16)

    x2d = x.reshape(n, D_MODEL)
    main, small = _in_proj(x2d, attn_norm.reshape(1, D_MODEL), w_main, w_small)

    main3 = main.reshape(bsz, t_len, MAIN_W)
    small3 = small.reshape(bsz, t_len, LANES)
    rep = LANES // HEAD_DIM
    qh, kh, vh, iqh, ikn = _prep(main3, small3, _rope_tables(t_len),
                                 jnp.tile(q_norm.reshape(1, HEAD_DIM), (1, rep)),
                                 jnp.tile(k_norm.reshape(1, HEAD_DIM), (1, rep)),
                                 jnp.tile(idx_k_norm.reshape(1, IDX_DIM), (1, rep)))
    y_attn = _attention(qh, kh, vh, iqh, ikn, small3, min(TOPK_MAX, t_len // 4))

    y_ssd = _ssd(main3, small3, conv_w, conv_b.reshape(1, SSD_XBC), dt_bias.reshape(1, SSD_HEADS),
                 a_log.reshape(1, SSD_HEADS), jnp.repeat(d_skip, SSD_HEAD_DIM).reshape(1, SSD_D_INNER),
                 ssd_norm.reshape(1, SSD_D_INNER))

    n_route = N_EXPERT_GROUPS + N_EXPERTS
    w_route = jnp.concatenate([w_route_group, w_route_expert, jnp.zeros((D_MODEL, LANES - n_route), f32)], axis=1)
    w_route_hi = w_route.astype(bf16)
    w_route = jnp.concatenate([w_route_hi, (w_route - w_route_hi.astype(f32)).astype(bf16)], axis=1)
    b_route = jnp.concatenate([b_route_group, b_route_expert, jnp.zeros((LANES - n_route,), f32)]).reshape(1, LANES)
    ffn_gain = ffn_norm.reshape(1, D_MODEL)
    xm, counts = _merge(y_attn.reshape(n, Q_W), y_ssd.reshape(n, SSD_D_INNER), main, x2d,
                        w_attn_branch.astype(bf16), w_ssd_branch.astype(bf16), w_out.astype(bf16),
                        ffn_gain, w_route, b_route)

    wgu = jnp.concatenate([w_gate, w_up], axis=-1).reshape(N_EXPERTS, D_MODEL, 2 * EXPERT_HIDDEN).astype(bf16)
    wd = w_down.reshape(N_EXPERTS, EXPERT_HIDDEN, D_MODEL).astype(bf16)
    out = _routed_moe(xm, counts, ffn_gain, wgu, wd)
    return out.reshape(bsz, t_len, D_MODEL)


def kernel(x, attn_norm, w_in, q_norm, k_norm, idx_k_norm, conv_w, conv_b, dt_bias, a_log, d_skip, ssd_norm,
           w_attn_branch, w_ssd_branch, w_out, ffn_norm, w_route_group, b_route_group, w_route_expert,
           b_route_expert, w_gate, w_up, w_down):
    for l in range(attn_norm.shape[0]):
        x = _layer(x, attn_norm[l], w_in[l], q_norm[l], k_norm[l], idx_k_norm[l], conv_w[l], conv_b[l], dt_bias[l],
                   a_log[l], d_skip[l], ssd_norm[l], w_attn_branch[l], w_ssd_branch[l], w_out[l], ffn_norm[l],
                   w_route_group[l], b_route_group[l], w_route_expert[l], b_route_expert[l], w_gate[l], w_up[l],
                   w_down[l])
    return x
```

```python
import functools
import math

import numpy as np
import jax
import jax.numpy as jnp
from jax import lax
from jax.experimental import pallas as pl
from jax.experimental.pallas import tpu as pltpu

f32 = jnp.float32
bf16 = jnp.bfloat16

D_MODEL = 1024
CHUNK = 64
Q_BLOCK = 128
EPS = 1e-6
N_HEADS = 16
N_KV_HEADS = 4
HEAD_DIM = 64
ROPE_DIM = HEAD_DIM // 4
ROPE_THETA = 500000.0
IDX_HEADS = 8
IDX_DIM = 64
TOPK_MAX = 256
SSD_D_INNER = 2 * D_MODEL
SSD_HEAD_DIM = 64
SSD_HEADS = SSD_D_INNER // SSD_HEAD_DIM
SSD_GROUPS = 4
SSD_STATE = 128
SSD_CONV = 4
SSD_BC = SSD_GROUPS * SSD_STATE
SSD_XBC = SSD_D_INNER + 2 * SSD_BC
N_EXPERT_GROUPS = 4
EXPERTS_PER_GROUP = 8
N_EXPERTS = N_EXPERT_GROUPS * EXPERTS_PER_GROUP
EXPERT_HIDDEN = 256
Q_W = N_HEADS * HEAD_DIM
KV_W = N_KV_HEADS * HEAD_DIM
IQ_W = IDX_HEADS * IDX_DIM
SPLITS = (Q_W, KV_W, KV_W, IQ_W, IDX_DIM, IDX_HEADS, SSD_D_INNER, SSD_XBC, SSD_HEADS, D_MODEL, D_MODEL)

LANES = 128
MAIN_W = Q_W + 2 * KV_W + IQ_W + SSD_D_INNER + SSD_XBC + 2 * D_MODEL
COL_Z = Q_W + 2 * KV_W + IQ_W
COL_XBC = COL_Z + SSD_D_INNER
COL_GA = COL_XBC + SSD_XBC
COL_GB = COL_GA + D_MODEL
SM_IW = IDX_DIM
SM_DT = IDX_DIM + IDX_HEADS
SSD_CHUNK = 128
VMEM_LIMIT = 56 * 1024 * 1024
TOKENS_IN_PROJ = 2048
TOKENS_PREP = 512
TOKENS_MERGE = 512
MOVE_UNROLL = 8
MOE_BLOCK = 1024
ROW_W = D_MODEL + LANES
META_GROUP = EXPERTS_PER_GROUP
META_RANK = EXPERTS_PER_GROUP + 1
INT_MIN = -(2 ** 31)


def _cparams(*sem):
    return pltpu.CompilerParams(dimension_semantics=sem, vmem_limit_bytes=VMEM_LIMIT)


def _split_dot(a, b, parts):
    acc = None
    rem = a
    for _ in range(parts):
        piece = rem.astype(bf16)
        d = jnp.dot(piece, b, preferred_element_type=f32)
        acc = d if acc is None else acc + d
        rem = rem - piece.astype(f32)
    return acc


def _inproj_kernel(x_ref, g_ref, w_ref, ws_ref, o_ref, os_ref, h_ref):
    @pl.when(pl.program_id(1) == 0)
    def _():
        x = x_ref[...]
        ms = jnp.mean(x * x, axis=-1, keepdims=True)
        hb = (x * lax.rsqrt(ms + EPS) * g_ref[...]).astype(bf16)
        h_ref[...] = hb
        os_ref[...] = jnp.dot(hb, ws_ref[...], preferred_element_type=f32)

    o_ref[...] = jnp.dot(h_ref[...], w_ref[...], preferred_element_type=f32).astype(bf16)


def _in_proj(x2d, gain, w_main, w_small):
    n = x2d.shape[0]
    tm = min(TOKENS_IN_PROJ, n)
    tn = 1024
    return pl.pallas_call(
        _inproj_kernel,
        grid=(n // tm, MAIN_W // tn),
        in_specs=[
            pl.BlockSpec((tm, D_MODEL), lambda i, j: (i, 0)),
            pl.BlockSpec((1, D_MODEL), lambda i, j: (0, 0)),
            pl.BlockSpec((D_MODEL, tn), lambda i, j: (0, j)),
            pl.BlockSpec((D_MODEL, LANES), lambda i, j: (0, 0)),
        ],
        out_specs=[
            pl.BlockSpec((tm, tn), lambda i, j: (i, j)),
            pl.BlockSpec((tm, LANES), lambda i, j: (i, 0)),
        ],
        out_shape=[
            jax.ShapeDtypeStruct((n, MAIN_W), bf16),
            jax.ShapeDtypeStruct((n, LANES), f32),
        ],
        scratch_shapes=[pltpu.VMEM((tm, D_MODEL), bf16)],
        compiler_params=_cparams("parallel", "arbitrary"),
        name="in_proj",
    )(x2d, gain, w_main, w_small)


def _rope_tables(t):
    half = ROPE_DIM // 2
    inv = ROPE_THETA ** (-jnp.arange(half, dtype=f32) * 2.0 / ROPE_DIM)
    ang = jnp.arange(t, dtype=f32)[:, None] * inv[None, :]
    cos, sin = jnp.cos(ang), jnp.sin(ang)
    pad = HEAD_DIM - ROPE_DIM
    c = jnp.concatenate([cos, cos, jnp.ones((t, pad), f32)], axis=1)
    s_lo = jnp.concatenate([-sin, jnp.zeros((t, half + pad), f32)], axis=1)
    s_hi = jnp.concatenate([jnp.zeros((t, half), f32), sin, jnp.zeros((t, pad), f32)], axis=1)
    rep = LANES // HEAD_DIM
    return jnp.tile(c, (1, rep)), jnp.tile(s_lo, (1, rep)), jnp.tile(s_hi, (1, rep))


def _rope(x, c, s_lo, s_hi):
    half = ROPE_DIM // 2
    return x * c + pltpu.roll(x, LANES - half, axis=1) * s_lo + pltpu.roll(x, half, axis=1) * s_hi


def _prep_kernel(q_ref, k_ref, v_ref, iq_ref, sm_ref, c_ref, slo_ref, shi_ref, qg_ref, kg_ref, ikg_ref,
                 qo_ref, ko_ref, vo_ref, iqo_ref, iko_ref):
    c, s_lo, s_hi = c_ref[...], slo_ref[...], shi_ref[...]
    r = lax.broadcasted_iota(jnp.int32, (LANES, LANES), 0) // HEAD_DIM
    cc = lax.broadcasted_iota(jnp.int32, (LANES, LANES), 1) // HEAD_DIM
    gsum = jnp.where(r == cc, 1.0, 0.0).astype(bf16)

    def norm_rope(x, gain, scale):
        ms = _split_dot(x * x, gsum, 2) * (1.0 / HEAD_DIM)
        return _rope(x * gain, c, s_lo, s_hi) * (lax.rsqrt(ms + EPS) * scale)

    def store_heads(o_ref, t, y):
        yb = y.astype(bf16)
        o_ref[0, 2 * t] = yb[:, :HEAD_DIM]
        o_ref[0, 2 * t + 1] = yb[:, HEAD_DIM:]

    for t in range(Q_W // LANES):
        sl = slice(t * LANES, (t + 1) * LANES)
        store_heads(qo_ref, t, norm_rope(q_ref[0, :, sl].astype(f32), qg_ref[...], HEAD_DIM ** -0.5))
    for t in range(KV_W // LANES):
        sl = slice(t * LANES, (t + 1) * LANES)
        store_heads(ko_ref, t, norm_rope(k_ref[0, :, sl].astype(f32), kg_ref[...], 1.0))
        store_heads(vo_ref, t, v_ref[0, :, sl])
    for t in range(IQ_W // LANES):
        sl = slice(t * LANES, (t + 1) * LANES)
        store_heads(iqo_ref, t, _rope(iq_ref[0, :, sl].astype(f32), c, s_lo, s_hi) * (IDX_DIM ** -0.5))
    ik = norm_rope(sm_ref[0], ikg_ref[...], 1.0)
    iko_ref[0] = ik[:, :IDX_DIM].astype(bf16)


def _prep(main3, small3, tables, q_gain, k_gain, ik_gain):
    bsz, t_len, _ = main3.shape
    tp = min(TOKENS_PREP, t_len)
    row = lambda w, col: pl.BlockSpec((1, tp, w), lambda b, i, col=col: (b, i, col))
    heads = lambda nh: pl.BlockSpec((1, nh, tp, HEAD_DIM), lambda b, i: (b, 0, i, 0))
    tab = pl.BlockSpec((tp, LANES), lambda b, i: (i, 0))
    par = pl.BlockSpec((1, LANES), lambda b, i: (0, 0))
    hshape = lambda nh: jax.ShapeDtypeStruct((bsz, nh, t_len, HEAD_DIM), bf16)
    return pl.pallas_call(
        _prep_kernel,
        grid=(bsz, t_len // tp),
        in_specs=[row(Q_W, 0), row(KV_W, Q_W // KV_W), row(KV_W, (Q_W + KV_W) // KV_W),
                  row(IQ_W, (Q_W + 2 * KV_W) // IQ_W), row(LANES, 0), tab, tab, tab, par, par, par],
        out_specs=[heads(N_HEADS), heads(N_KV_HEADS), heads(N_KV_HEADS), heads(IDX_HEADS),
                   pl.BlockSpec((1, tp, IDX_DIM), lambda b, i: (b, i, 0))],
        out_shape=[hshape(N_HEADS), hshape(N_KV_HEADS), hshape(N_KV_HEADS), hshape(IDX_HEADS),
                   jax.ShapeDtypeStruct((bsz, t_len, IDX_DIM), bf16)],
        compiler_params=_cparams("parallel", "parallel"),
        name="prep",
    )(main3, main3, main3, main3, small3, *tables, q_gain, k_gain, ik_gain)


def _select_topk(score, topk):
    s, cols = score.shape

    def count(m):
        ones = jnp.where(m, 1.0, 0.0).reshape(s // Q_BLOCK, Q_BLOCK, cols)
        return jnp.sum(jnp.sum(ones, axis=0), axis=0, keepdims=True)

    def as_float(image):
        return lax.bitcast_convert_type(jnp.where(image >= 0, image, image ^ 0x7FFFFFFF), f32)

    t0 = jnp.where(count(score >= 0.0) >= topk, 0, INT_MIN).astype(jnp.int32)

    def vbit(i, t):
        cand = t + jnp.left_shift(jnp.int32(1), 30 - i)
        return jnp.where(count(score >= as_float(cand)) >= topk, cand, t)

    thr = as_float(lax.fori_loop(0, 31, vbit, t0))
    above = score > thr
    ties = score == thr
    need = topk - count(above)
    idx = lax.broadcasted_iota(jnp.int32, (s, cols), 0)
    nbits = max(1, int(math.ceil(math.log2(s))))

    def ibit(i, m):
        cand = m + jnp.left_shift(jnp.int32(1), nbits - 1 - i)
        return jnp.where(count(ties & (idx < cand)) < need, cand, m)

    def tie_break():
        return lax.fori_loop(0, nbits, ibit, jnp.zeros((1, cols), jnp.int32))

    surplus = jnp.max(jnp.where(count(ties) > need, 1.0, 0.0)) > 0.0
    m = lax.cond(surplus, tie_break, lambda: jnp.full((1, cols), s, jnp.int32))
    return above | (ties & (idx <= m))


def _attn_kernel(q_ref, iq_ref, sm_ref, k_ref, v_ref, ik_ref, y_hbm_ref, o_ref, *, s_len, blk, topk):
    del y_hbm_ref
    nt = (((1,), (1,)), ((), ()))
    ik = ik_ref[0]
    iw_t = sm_ref[0].T[SM_IW:SM_IW + IDX_HEADS, :] * (IDX_HEADS ** -0.5)
    score = jnp.zeros((s_len, Q_BLOCK), f32)
    for hp in range(IDX_HEADS // 2):
        iq2 = iq_ref[0, 2 * hp:2 * hp + 2].reshape(2 * Q_BLOCK, IDX_DIM)
        rel = lax.dot_general(ik, iq2, nt, preferred_element_type=f32)
        score = (score + jnp.maximum(rel[:, :Q_BLOCK], 0.0) * iw_t[2 * hp:2 * hp + 1]
                 + jnp.maximum(rel[:, Q_BLOCK:], 0.0) * iw_t[2 * hp + 1:2 * hp + 2])

    k_chunk = lax.broadcasted_iota(jnp.int32, (s_len, Q_BLOCK), 0) // CHUNK
    q_chunk = lax.broadcasted_iota(jnp.int32, (s_len, Q_BLOCK), 1) // CHUNK + blk * (Q_BLOCK // CHUNK)
    adm = k_chunk <= q_chunk
    if s_len <= topk:
        sel = adm
    else:
        sel = _select_topk(jnp.where(adm, score, -jnp.inf), topk)
    bias = jnp.where(sel, 0.0, -jnp.inf).T

    grp = N_HEADS // N_KV_HEADS
    for g in range(N_KV_HEADS):
        qg = q_ref[0, g * grp:(g + 1) * grp].reshape(grp * Q_BLOCK, HEAD_DIM)
        logits = lax.dot_general(qg, k_ref[0, g], nt, preferred_element_type=f32)
        logits = logits.reshape(grp, Q_BLOCK, s_len) + bias[None]
        mx = jnp.max(logits, axis=2, keepdims=True)
        p = jnp.exp(logits - mx)
        den = jnp.sum(p, axis=2, keepdims=True)
        o = jnp.dot(p.reshape(grp * Q_BLOCK, s_len).astype(bf16), v_ref[0, g], preferred_element_type=f32)
        o = o.reshape(grp, Q_BLOCK, HEAD_DIM) / den
        for i in range(grp):
            h = g * grp + i
            o_ref[0, :, h * HEAD_DIM:(h + 1) * HEAD_DIM] = o[i].astype(bf16)


def _attention(qh, kh, vh, iqh, ikn, small3, topk):
    bsz, _, t_len, _ = qh.shape
    assert topk % CHUNK == 0
    y = jnp.zeros((bsz, t_len, Q_W), bf16)
    for blk in range(t_len // Q_BLOCK):
        s_len = (blk + 1) * Q_BLOCK
        qhead = lambda nh, blk=blk: pl.BlockSpec((1, nh, Q_BLOCK, HEAD_DIM), lambda b: (b, 0, blk, 0))
        khead = lambda nh, s_len=s_len: pl.BlockSpec((1, nh, s_len, HEAD_DIM), lambda b: (b, 0, 0, 0))
        in_specs = [qhead(N_HEADS), qhead(IDX_HEADS), pl.BlockSpec((1, Q_BLOCK, LANES), lambda b, blk=blk: (b, blk, 0)),
                    khead(N_KV_HEADS), khead(N_KV_HEADS), pl.BlockSpec((1, s_len, IDX_DIM), lambda b: (b, 0, 0)),
                    pl.BlockSpec(memory_space=pl.ANY)]
        y = pl.pallas_call(
            functools.partial(_attn_kernel, s_len=s_len, blk=blk, topk=topk),
            grid=(bsz,),
            in_specs=in_specs,
            out_specs=pl.BlockSpec((1, Q_BLOCK, Q_W), lambda b, blk=blk: (b, blk, 0)),
            out_shape=jax.ShapeDtypeStruct((bsz, t_len, Q_W), bf16),
            input_output_aliases={len(in_specs) - 1: 0},
            compiler_params=_cparams("parallel"),
            name=f"attn_{blk}",
        )(qh, iqh, small3, kh, vh, ikn, y)
    return y


def _ssd_kernel(xa_ref, xb_ref, xc_ref, z_ref, sm_ref, cw_ref, cb_ref, dtb_ref, alog_ref, dsk_ref, gn_ref,
                o_ref, state_ref, carry_ref, *, clen):
    @pl.when(pl.program_id(1) == 0)
    def _():
        state_ref[...] = jnp.zeros_like(state_ref)
        carry_ref[...] = jnp.zeros_like(carry_ref)

    raw = jnp.concatenate([xa_ref[0], xb_ref[0], xc_ref[0]], axis=1)
    both = jnp.concatenate([carry_ref[...], raw], axis=0)
    carry_ref[...] = raw
    pr = lax.broadcasted_iota(jnp.int32, ((SSD_CONV - 1) * clen, 2 * clen), 0)
    pc = lax.broadcasted_iota(jnp.int32, ((SSD_CONV - 1) * clen, 2 * clen), 1)
    pick = jnp.where(pc == clen + pr % clen - (pr // clen + 1), 1.0, 0.0).astype(bf16)
    shifted = jnp.dot(pick, both, preferred_element_type=f32)
    acc = raw.astype(f32) * cw_ref[SSD_CONV - 1:SSD_CONV, :] + cb_ref[...]
    for s in range(1, SSD_CONV):
        acc = acc + shifted[(s - 1) * clen:s * clen] * cw_ref[SSD_CONV - 1 - s:SSD_CONV - s, :]
    act = acc * jax.nn.sigmoid(acc)
    xs = act[:, :SSD_D_INNER]
    bm = act[:, SSD_D_INNER:SSD_D_INNER + SSD_BC].astype(bf16)
    cm = act[:, SSD_D_INNER + SSD_BC:].astype(bf16)

    dtr = sm_ref[0][:, SM_DT:SM_DT + SSD_HEADS] + dtb_ref[...]
    dt = jnp.maximum(dtr, 0.0) + jnp.log1p(jnp.exp(-jnp.abs(dtr)))
    a = dt * (-jnp.exp(alog_ref[...]))
    ri = lax.broadcasted_iota(jnp.int32, (clen, clen), 0)
    ci = lax.broadcasted_iota(jnp.int32, (clen, clen), 1)
    causal = ci <= ri
    tri = jnp.where(causal, 1.0, 0.0).astype(bf16)
    acs = None
    rem = a
    for _ in range(3):
        piece = rem.astype(bf16)
        d = jnp.dot(tri, piece, preferred_element_type=f32)
        acs = d if acs is None else acs + d
        rem = rem - piece.astype(f32)
    a_last = acs[clen - 1:clen, :]
    acs_t = jnp.concatenate([acs, jnp.zeros((clen, LANES - SSD_HEADS), f32)], axis=1).T

    hh = lax.broadcasted_iota(jnp.int32, (SSD_HEADS, SSD_D_INNER), 0)
    ch = lax.broadcasted_iota(jnp.int32, (SSD_HEADS, SSD_D_INNER), 1) // SSD_HEAD_DIM
    rep = jnp.where(hh == ch, 1.0, 0.0).astype(bf16)
    cd_rows = jnp.broadcast_to(jnp.exp(a_last), (8, SSD_HEADS))
    fac = _split_dot(jnp.concatenate([dt, jnp.exp(a_last - acs), jnp.exp(acs), cd_rows], axis=0), rep, 2)
    dt_e, dte_e, eacs_e, cd_e = fac[:clen], fac[clen:2 * clen], fac[2 * clen:3 * clen], fac[3 * clen:3 * clen + 1]

    xdt = xs * dt_e
    xdt_b = xdt.astype(bf16)
    xw_b = (xdt * dte_e).astype(bf16)
    z = z_ref[0].astype(f32)
    zg = z * jax.nn.sigmoid(z)
    gw = SSD_D_INNER // SSD_GROUPS
    hpg = SSD_HEADS // SSD_GROUPS
    for g in range(SSD_GROUPS):
        gs = slice(g * gw, (g + 1) * gw)
        bg = bm[:, g * SSD_STATE:(g + 1) * SSD_STATE]
        cg = cm[:, g * SSD_STATE:(g + 1) * SSD_STATE]
        cb = lax.dot_general(cg, bg, (((1,), (1,)), ((), ())), preferred_element_type=f32)
        hprev = state_ref[g]
        y_off = jnp.dot(cg, hprev.astype(bf16), preferred_element_type=f32)
        snew = lax.dot_general(bg, xw_b[:, gs], (((0,), (0,)), ((), ())), preferred_element_type=f32)
        state_ref[g] = cd_e[:, gs] * hprev + snew
        parts = []
        for r in range(hpg):
            h = g * hpg + r
            seg = acs[:, h:h + 1] - acs_t[h:h + 1, :]
            dec = jnp.exp(jnp.where(causal, seg, -jnp.inf))
            parts.append(jnp.dot((cb * dec).astype(bf16), xdt_b[:, h * SSD_HEAD_DIM:(h + 1) * SSD_HEAD_DIM],
                                 preferred_element_type=f32))
        y = jnp.concatenate(parts, axis=1) + y_off * eacs_e[:, gs] + dsk_ref[:, gs] * xs[:, gs]
        yz = y * zg[:, gs]
        ms = jnp.mean(yz * yz, axis=-1, keepdims=True)
        o_ref[0, :, gs] = (yz * lax.rsqrt(ms + EPS) * gn_ref[:, gs]).astype(bf16)


def _ssd(main3, small3, conv_w, conv_b, dt_bias, a_log, d_skip_e, ssd_norm):
    bsz, t_len, _ = main3.shape
    clen = min(SSD_CHUNK, t_len)
    xw = SSD_XBC // 3
    xcol = COL_XBC // xw
    blk = lambda w, col: pl.BlockSpec((1, clen, w), lambda b, c, col=col: (b, c, col))
    par = lambda r, w: pl.BlockSpec((r, w), lambda b, c: (0, 0))
    return pl.pallas_call(
        functools.partial(_ssd_kernel, clen=clen),
        grid=(bsz, t_len // clen),
        in_specs=[blk(xw, xcol), blk(xw, xcol + 1), blk(xw, xcol + 2), blk(SSD_D_INNER, COL_Z // SSD_D_INNER),
                  blk(LANES, 0), par(SSD_CONV, SSD_XBC), par(1, SSD_XBC), par(1, SSD_HEADS), par(1, SSD_HEADS),
                  par(1, SSD_D_INNER), par(1, SSD_D_INNER)],
        out_specs=pl.BlockSpec((1, clen, SSD_D_INNER), lambda b, c: (b, c, 0)),
        out_shape=jax.ShapeDtypeStruct((bsz, t_len, SSD_D_INNER), bf16),
        scratch_shapes=[pltpu.VMEM((SSD_GROUPS, SSD_STATE, SSD_D_INNER // SSD_GROUPS), f32),
                        pltpu.VMEM((clen, SSD_XBC), bf16)],
        compiler_params=_cparams("parallel", "arbitrary"),
        name="ssd",
    )(main3, main3, main3, main3, small3, conv_w, conv_b, dt_bias, a_log, d_skip_e, ssd_norm)


def _merge_kernel(ya_ref, ys_ref, ga_ref, gb_ref, x_ref, wa_ref, wb_ref, wo_ref, fg_ref, wr_ref, br_ref,
                  xm_ref, rec_ref, cnt_ref, carry_ref):
    @pl.when(pl.program_id(0) == 0)
    def _():
        carry_ref[...] = jnp.zeros_like(carry_ref)

    a = jnp.dot(ya_ref[...], wa_ref[...], preferred_element_type=f32)
    s = jnp.dot(ys_ref[...], wb_ref[...], preferred_element_type=f32)
    merged = jax.nn.sigmoid(ga_ref[...].astype(f32)) * a + jax.nn.sigmoid(gb_ref[...].astype(f32)) * s
    x2 = x_ref[...] + jnp.dot(merged.astype(bf16), wo_ref[...], preferred_element_type=f32)
    xm_ref[:, :D_MODEL] = x2
    ms = jnp.mean(x2 * x2, axis=-1, keepdims=True)
    hn = x2 * lax.rsqrt(ms + EPS) * fg_ref[...]
    hn_hi = hn.astype(bf16)
    hn_lo = (hn - hn_hi.astype(f32)).astype(bf16)
    hw = jnp.dot(hn_hi, wr_ref[...], preferred_element_type=f32)
    logits = (hw[:, :LANES] + hw[:, LANES:] + jnp.dot(hn_lo, wr_ref[:, :LANES], preferred_element_type=f32)
              + br_ref[...])
    lane = lax.broadcasted_iota(jnp.int32, logits.shape, 1)
    neg = -jnp.inf
    big = jnp.int32(LANES)

    def first_max(mask):
        v = jnp.max(jnp.where(mask, logits, neg), axis=1, keepdims=True)
        i = jnp.min(jnp.where(mask & (logits == v), lane, big), axis=1, keepdims=True)
        return v, i

    is_g = lane < N_EXPERT_GROUPS
    gmax, gidx = first_max(is_g)
    g_w = 1.0 / jnp.sum(jnp.where(is_g, jnp.exp(logits - gmax), 0.0), axis=1, keepdims=True)
    e_lane = lane - N_EXPERT_GROUPS
    in_grp = (e_lane >= gidx * EXPERTS_PER_GROUP) & (e_lane < (gidx + 1) * EXPERTS_PER_GROUP)
    v1, i1 = first_max(in_grp)
    v2, i2 = first_max(in_grp & (lane != i1))
    e2 = jnp.exp(v2 - v1)
    p1 = 1.0 / (1.0 + e2)
    p2 = e2 / (1.0 + e2)
    cmb = jnp.where(lane == i1, g_w * p1, 0.0) + jnp.where(lane == i2, g_w * p2, 0.0)
    cmb8 = jnp.zeros_like(cmb)
    for g in range(N_EXPERT_GROUPS):
        first = N_EXPERT_GROUPS + g * EXPERTS_PER_GROUP
        cmb8 = cmb8 + jnp.where(gidx == g, pltpu.roll(cmb, LANES - first, axis=1), 0.0)
    in_g = jnp.where(lane == gidx, 1.0, 0.0)
    tm = in_g.shape[0]
    earlier = jnp.where(lax.broadcasted_iota(jnp.int32, (tm, tm), 1) < lax.broadcasted_iota(jnp.int32, (tm, tm), 0),
                        1.0, 0.0).astype(bf16)
    before = jnp.dot(earlier, in_g.astype(bf16), preferred_element_type=f32) + carry_ref[0:1, :]
    rank = jnp.sum(in_g * before, axis=1, keepdims=True)
    carry_ref[...] = carry_ref[...] + jnp.sum(in_g, axis=0, keepdims=True)
    cnt_ref[...] = carry_ref[...]
    record = cmb8 + jnp.where(lane == META_GROUP, gidx.astype(f32), 0.0) + jnp.where(lane == META_RANK, rank, 0.0)
    xm_ref[:, D_MODEL:] = record
    rec_ref[...] = record


def _merge(y_attn, y_ssd, main, x2d, wa, wb, wo, ffn_gain, w_route, b_route):
    n = x2d.shape[0]
    tm = min(TOKENS_MERGE, n)
    row = lambda w, col=0: pl.BlockSpec((tm, w), lambda i, col=col: (i, col))
    full = lambda r, w: pl.BlockSpec((r, w), lambda i: (0, 0))
    return pl.pallas_call(
        _merge_kernel,
        grid=(n // tm,),
        in_specs=[row(Q_W), row(SSD_D_INNER), row(D_MODEL, COL_GA // D_MODEL), row(D_MODEL, COL_GB // D_MODEL),
                  row(D_MODEL), full(Q_W, D_MODEL), full(SSD_D_INNER, D_MODEL), full(D_MODEL, D_MODEL),
                  full(1, D_MODEL), full(D_MODEL, 2 * LANES), full(1, LANES)],
        out_specs=[row(ROW_W), row(LANES), full(8, LANES)],
        out_shape=[
            jax.ShapeDtypeStruct((n, ROW_W), f32),
            jax.ShapeDtypeStruct((n, LANES), f32),
            jax.ShapeDtypeStruct((8, LANES), f32),
        ],
        scratch_shapes=[pltpu.VMEM((8, LANES), f32)],
        compiler_params=_cparams("arbitrary"),
        name="merge",
    )(y_attn, y_ssd, main, main, x2d, wa, wb, wo, ffn_gain, w_route, b_route)


def _invert_kernel(pos_ref, src_ref, *, n_tokens, n_slots):
    def clear(i, carry):
        src_ref[i] = 0
        return carry

    def put(t, carry):
        src_ref[pos_ref[t]] = t
        return carry

    lax.fori_loop(0, n_slots, clear, 0, unroll=MOVE_UNROLL)
    lax.fori_loop(0, n_tokens, put, 0, unroll=MOVE_UNROLL)


def _invert(pos, n_slots):
    smem = pl.BlockSpec(memory_space=pltpu.SMEM)
    return pl.pallas_call(
        functools.partial(_invert_kernel, n_tokens=pos.shape[0], n_slots=n_slots),
        in_specs=[smem], out_specs=smem,
        out_shape=jax.ShapeDtypeStruct((n_slots,), jnp.int32),
        name="moe_slots",
    )(pos)


def _moe_kernel(grp_ref, nused_ref, nvalid_ref, src_ref, xm_hbm, fg_ref, wg_ref, wu_ref, wd_ref, out_hbm,
                rows_ref, acc_ref, hn_ref, sem_in, sem_out, *, blk):
    del grp_ref
    b = pl.program_id(0)
    e = pl.program_id(1)
    base = b * blk
    n_valid = nvalid_ref[b]

    def fetch(j):
        return pltpu.make_async_copy(xm_hbm.at[pl.ds(src_ref[base + j], 1)], rows_ref.at[pl.ds(j, 1)], sem_in)

    def emit(j):
        return pltpu.make_async_copy(acc_ref.at[pl.ds(j, 1)], out_hbm.at[pl.ds(src_ref[base + j], 1)], sem_out)

    def for_rows(count, fn):
        def chunk(c, carry):
            for i in range(MOVE_UNROLL):
                fn(c * MOVE_UNROLL + i)
            return carry

        def one(j, carry):
            fn(j)
            return carry

        full = count // MOVE_UNROLL
        lax.fori_loop(0, full, chunk, 0)
        lax.fori_loop(full * MOVE_UNROLL, count, one, 0)

    @pl.when(b < nused_ref[0])
    def _():
        valid = lax.broadcasted_iota(jnp.int32, (blk, 1), 0) < n_valid

        @pl.when(e == 0)
        def _():
            for_rows(blk, lambda j: fetch(j).start())
            for_rows(blk, lambda j: fetch(j).wait())
            x2 = jnp.where(valid, rows_ref[:, :D_MODEL], 0.0)
            acc_ref[...] = x2
            ms = jnp.mean(x2 * x2, axis=-1, keepdims=True)
            hn_ref[...] = (x2 * lax.rsqrt(ms + EPS) * fg_ref[...]).astype(bf16)

        meta = rows_ref[:, D_MODEL:]
        lane = lax.broadcasted_iota(jnp.int32, meta.shape, 1)
        scale = jnp.sum(jnp.where(valid & (lane == e), meta, 0.0), axis=1, keepdims=True)
        hn = hn_ref[...]
        a = jnp.dot(hn, wg_ref[0].astype(bf16), preferred_element_type=f32)
        u = jnp.dot(hn, wu_ref[0].astype(bf16), preferred_element_type=f32)
        act = a * jax.nn.sigmoid(a) * u * scale
        acc_ref[...] += jnp.dot(act.astype(bf16), wd_ref[0].astype(bf16), preferred_element_type=f32)

        @pl.when(e == EXPERTS_PER_GROUP - 1)
        def _():
            for_rows(n_valid, lambda j: emit(j).start())
            for_rows(n_valid, lambda j: emit(j).wait())


def _moe(xm, blk, block_group, n_used, n_valid, src, ffn_gain, wg, wu, wd):
    n = xm.shape[0]
    expert = lambda b, e, grp, *_: (grp[b] * EXPERTS_PER_GROUP + e, 0, 0)
    return pl.pallas_call(
        functools.partial(_moe_kernel, blk=blk),
        grid_spec=pltpu.PrefetchScalarGridSpec(
            num_scalar_prefetch=4,
            grid=(src.shape[0] // blk, EXPERTS_PER_GROUP),
            in_specs=[
                pl.BlockSpec(memory_space=pl.ANY),
                pl.BlockSpec((1, D_MODEL), lambda b, e, *_: (0, 0)),
                pl.BlockSpec((1, D_MODEL, EXPERT_HIDDEN), expert),
                pl.BlockSpec((1, D_MODEL, EXPERT_HIDDEN), expert),
                pl.BlockSpec((1, EXPERT_HIDDEN, D_MODEL), expert),
            ],
            out_specs=pl.BlockSpec(memory_space=pl.ANY),
            scratch_shapes=[pltpu.VMEM((blk, ROW_W), f32), pltpu.VMEM((blk, D_MODEL), f32),
                            pltpu.VMEM((blk, D_MODEL), bf16),
                            pltpu.SemaphoreType.DMA(()), pltpu.SemaphoreType.DMA(())]),
        out_shape=jax.ShapeDtypeStruct((n, D_MODEL), f32),
        compiler_params=_cparams("arbitrary", "arbitrary"),
        name="moe",
    )(block_group, n_used, n_valid, src, xm, ffn_gain, wg, wu, wd)


def _routed_moe(xm, record, counts, ffn_gain, wg, wu, wd):
    n = xm.shape[0]
    blk = min(MOE_BLOCK, n)
    group = record[:, META_GROUP].astype(jnp.int32)
    rank = record[:, META_RANK].astype(jnp.int32)
    cnt = counts[0, :N_EXPERT_GROUPS].astype(jnp.int32)
    padded = (cnt + blk - 1) // blk * blk
    ends = jnp.cumsum(padded)
    offs = ends - padded
    pos = offs[group] + rank
    n_blocks = n // blk + N_EXPERT_GROUPS
    starts = jnp.arange(n_blocks, dtype=jnp.int32) * blk
    block_group = jnp.minimum(jnp.sum(starts[:, None] >= ends[None, :], axis=1), N_EXPERT_GROUPS - 1).astype(jnp.int32)
    n_valid = jnp.clip(cnt[block_group] - (starts - offs[block_group]), 0, blk).astype(jnp.int32)
    n_used = (ends[-1:] // blk).astype(jnp.int32)
    src = _invert(pos, n_blocks * blk)
    return _moe(xm, blk, block_group, n_used, n_valid, src, ffn_gain, wg, wu, wd)

def _layer(x, attn_norm, w_in, q_norm, k_norm, idx_k_norm, conv_w, conv_b, dt_bias, a_log, d_skip, ssd_norm,
           w_attn_branch, w_ssd_branch, w_out, ffn_norm, w_route_group, b_route_group, w_route_expert,
           b_route_expert, w_gate, w_up, w_down):
    bsz, t_len, _ = x.shape
    n = bsz * t_len
    offs = np.cumsum(SPLITS)[:-1].tolist()
    (wq, wk, wv, wiq, wik, wiw, wz, wxbc, wdt, wga, wgb) = jnp.split(w_in, offs, axis=-1)
    w_main = jnp.concatenate([wq, wk, wv, wiq, wz, wxbc, wga, wgb], axis=1).astype(bf16)
    pad = LANES - (IDX_DIM + IDX_HEADS + SSD_HEADS)
    w_small = jnp.concatenate([wik, wiw, wdt, jnp.zeros((D_MODEL, pad), f32)], axis=1).astype(bf16)

    x2d = x.reshape(n, D_MODEL)
    main, small = _in_proj(x2d, attn_norm.reshape(1, D_MODEL), w_main, w_small)

    main3 = main.reshape(bsz, t_len, MAIN_W)
    small3 = small.reshape(bsz, t_len, LANES)
    rep = LANES // HEAD_DIM
    qh, kh, vh, iqh, ikn = _prep(main3, small3, _rope_tables(t_len),
                                 jnp.tile(q_norm.reshape(1, HEAD_DIM), (1, rep)),
                                 jnp.tile(k_norm.reshape(1, HEAD_DIM), (1, rep)),
                                 jnp.tile(idx_k_norm.reshape(1, IDX_DIM), (1, rep)))
    y_attn = _attention(qh, kh, vh, iqh, ikn, small3, min(TOPK_MAX, t_len // 4))

    y_ssd = _ssd(main3, small3, conv_w, conv_b.reshape(1, SSD_XBC), dt_bias.reshape(1, SSD_HEADS),
                 a_log.reshape(1, SSD_HEADS), jnp.repeat(d_skip, SSD_HEAD_DIM).reshape(1, SSD_D_INNER),
                 ssd_norm.reshape(1, SSD_D_INNER))

    n_route = N_EXPERT_GROUPS + N_EXPERTS
    w_route = jnp.concatenate([w_route_group, w_route_expert, jnp.zeros((D_MODEL, LANES - n_route), f32)], axis=1)
    w_route_hi = w_route.astype(bf16)
    w_route = jnp.concatenate([w_route_hi, (w_route - w_route_hi.astype(f32)).astype(bf16)], axis=1)
    b_route = jnp.concatenate([b_route_group, b_route_expert, jnp.zeros((LANES - n_route,), f32)]).reshape(1, LANES)
    ffn_gain = ffn_norm.reshape(1, D_MODEL)
    xm, record, counts = _merge(y_attn.reshape(n, Q_W), y_ssd.reshape(n, SSD_D_INNER), main, x2d,
                        w_attn_branch.astype(bf16), w_ssd_branch.astype(bf16), w_out.astype(bf16),
                        ffn_gain, w_route, b_route)

    out = _routed_moe(xm, record, counts, ffn_gain, w_gate.reshape(N_EXPERTS, D_MODEL, EXPERT_HIDDEN),
                      w_up.reshape(N_EXPERTS, D_MODEL, EXPERT_HIDDEN), w_down.reshape(N_EXPERTS, EXPERT_HIDDEN, D_MODEL))
    return out.reshape(bsz, t_len, D_MODEL)


def kernel(x, attn_norm, w_in, q_norm, k_norm, idx_k_norm, conv_w, conv_b, dt_bias, a_log, d_skip, ssd_norm,
           w_attn_branch, w_ssd_branch, w_out, ffn_norm, w_route_group, b_route_group, w_route_expert,
           b_route_expert, w_gate, w_up, w_down):
    for l in range(attn_norm.shape[0]):
        x = _layer(x, attn_norm[l], w_in[l], q_norm[l], k_norm[l], idx_k_norm[l], conv_w[l], conv_b[l], dt_bias[l],
                   a_log[l], d_skip[l], ssd_norm[l], w_attn_branch[l], w_ssd_branch[l], w_out[l], ffn_norm[l],
                   w_route_group[l], b_route_group[l], w_route_expert[l], b_route_expert[l], w_gate[l], w_up[l],
                   w_down[l])
    return x
```
